```python
import math
import jax, jax.numpy as jnp
from jax import lax
import numpy as np

D_MODEL = 1024
BATCH = 8
SEQ = 2048
DEPTH = 1
DEC_BATCH = 128
DEC_SEQ = 4
PAST_LEN = 16384
PAGE_SIZE = 128

S5_WIDTH = D_MODEL // 2
S5_GROUP = 16
S5_GROUPS = S5_WIDTH // S5_GROUP
S5_STATE = 64
CONV_WIDTH = D_MODEL // 2
CONV_K = 31
N_MEM = 256
XA_HEADS = 4
XA_HEAD_DIM = D_MODEL // 8
XA_WIDTH = XA_HEADS * XA_HEAD_DIM
N_BRANCH = 3
IN_WIDTH = S5_WIDTH + 2 * CONV_WIDTH + XA_WIDTH + N_BRANCH * D_MODEL
PEER_HEADS = 8
PEER_NKEYS = 128
PEER_N_EXPERTS = PEER_NKEYS * PEER_NKEYS
PEER_KEY_DIM = D_MODEL // 4
PEER_HALF = PEER_KEY_DIM // 2
PEER_TOPK = 16
PEER_BLOCK = 256
RMS_EPS = 1e-6
LN_EPS = 1e-5

kernel_name = 'hybrid_s5_conformer_memxattn_peer_step'


def _rmsnorm(x, g):
    xf = x.astype(jnp.float32)
    y = xf * lax.rsqrt(jnp.mean(xf * xf, axis=-1, keepdims=True) + RMS_EPS)
    return (y * g.astype(jnp.float32)).astype(x.dtype)


def _layernorm(x, g, b):
    xf = x.astype(jnp.float32)
    xc = xf - jnp.mean(xf, axis=-1, keepdims=True)
    var = jnp.mean(xc * xc, axis=-1, keepdims=True)
    y = xc * lax.rsqrt(var + LN_EPS) * g.astype(jnp.float32) + b.astype(jnp.float32)
    return y.astype(x.dtype)


def _s5_discretize(log_dt, a_re, a_im, b_re, b_im):
    f32 = jnp.float32
    lam = lax.complex(a_re.astype(f32), a_im.astype(f32))
    dt = jnp.exp(log_dt.astype(f32))[:, None]
    lam_bar = jnp.exp(lam * dt)
    b_mat = lax.complex(b_re.astype(f32), b_im.astype(f32))
    b_bar = ((lam_bar - 1.0) / lam)[:, :, None] * b_mat
    return lam_bar, b_bar


def _s5_scan(u, h0, lam_bar, b_bar, c_re, c_im, d):
    bsz, t = u.shape[0], u.shape[1]
    ug = u.astype(jnp.float32).reshape(bsz, t, S5_GROUPS, S5_GROUP)
    bu = jnp.einsum('gnc,btgc->btgn', b_bar, ug.astype(jnp.complex64))
    bu = bu.at[:, 0].add(lam_bar[None] * h0)
    a = jnp.broadcast_to(lam_bar, bu.shape)

    def combine(e1, e2):
        a1, b1 = e1
        a2, b2 = e2
        return a2 * a1, a2 * b1 + b2

    _, states = lax.associative_scan(combine, (a, bu), axis=1)
    c_mat = lax.complex(c_re.astype(jnp.float32), c_im.astype(jnp.float32))
    y = jnp.einsum('gcn,btgn->btgc', c_mat, states).real + d.astype(jnp.float32) * ug
    return y.reshape(bsz, t, S5_WIDTH).astype(u.dtype), states[:, -1]


def _conv_module(a, conv_state, dw_w, dw_b, ln_g, ln_b):
    xin = jnp.concatenate([conv_state.astype(a.dtype), a], axis=1)
    y = lax.conv_general_dilated(
        xin, dw_w.astype(a.dtype)[:, None, :], window_strides=(1,), padding='VALID',
        dimension_numbers=('NWC', 'WIO', 'NWC'), feature_group_count=CONV_WIDTH)
    y = _layernorm(y + dw_b.astype(a.dtype), ln_g, ln_b)
    return jax.nn.silu(y), xin[:, -(CONV_K - 1):]


def _mem_kv(mem, mem_norm_g, w_mem_k, w_mem_v):
    bsz = mem.shape[0]
    m = _rmsnorm(mem, mem_norm_g)
    k = (m @ w_mem_k).reshape(bsz, N_MEM, XA_HEADS, XA_HEAD_DIM)
    v = (m @ w_mem_v).reshape(bsz, N_MEM, XA_HEADS, XA_HEAD_DIM)
    return k, v


def _cross_attend(q, k, v):
    s = jnp.einsum('bthd,bmhd->bhtm', q, k).astype(jnp.float32) * (XA_HEAD_DIM ** -0.5)
    p = jax.nn.softmax(s, axis=-1).astype(v.dtype)
    return jnp.einsum('bhtm,bmhd->bthd', p, v)


def _peer_block(xb, w_q, sub_keys, u_tab, v_tab):
    tb = xb.shape[0]
    q = (xb @ w_q).reshape(tb, PEER_HEADS, 2, PEER_HALF).astype(jnp.float32)
    s = jnp.einsum('thpc,pkc->thpk', q, sub_keys.astype(jnp.float32))
    sv, si = lax.top_k(s, PEER_TOPK)
    cand = sv[:, :, 0, :, None] + sv[:, :, 1, None, :]
    cv, ci = lax.top_k(cand.reshape(tb, PEER_HEADS, PEER_TOPK * PEER_TOPK), PEER_TOPK)
    i1 = jnp.take_along_axis(si[:, :, 0], ci // PEER_TOPK, axis=-1)
    i2 = jnp.take_along_axis(si[:, :, 1], ci % PEER_TOPK, axis=-1)
    experts = i1 * PEER_NKEYS + i2
    gate = jax.nn.softmax(cv, axis=-1)
    u = u_tab[experts]
    act = jax.nn.gelu(jnp.einsum('thkd,td->thk', u, xb).astype(jnp.float32), approximate=False)
    w = (gate * act).astype(xb.dtype)
    return jnp.einsum('thk,thkd->td', w, v_tab[experts])


def _peer(x, w_q, sub_keys, u_tab, v_tab):
    bsz, t, d = x.shape
    n = bsz * t
    nblk = -(-n // PEER_BLOCK)
    flat = jnp.pad(x.reshape(n, d), ((0, nblk * PEER_BLOCK - n), (0, 0)))
    out = lax.map(lambda xb: _peer_block(xb, w_q, sub_keys, u_tab, v_tab),
                  flat.reshape(nblk, PEER_BLOCK, d))
    return out.reshape(nblk * PEER_BLOCK, d)[:n].reshape(bsz, t, d)


def _trunk_layer(h, mem_k, mem_v, ssm_h0, conv_state,
                 norm1_g, w_in, b_in, s5_log_dt, s5_a_re, s5_a_im, s5_b_re, s5_b_im,
                 s5_c_re, s5_c_im, s5_d, s5_glu_w, s5_glu_b, w_s5_proj,
                 conv_dw_w, conv_dw_b, conv_ln_g, conv_ln_b, w_conv_proj,
                 w_xa_proj, w_out, norm2_g, peer_w_q, peer_sub_keys, peer_u, peer_v):
    bsz, t, d = h.shape
    n = _rmsnorm(h, norm1_g)
    z = n @ w_in + b_in
    o1 = S5_WIDTH
    o2 = o1 + 2 * CONV_WIDTH
    o3 = o2 + XA_WIDTH
    u_s5, a_pre, q, gl = z[..., :o1], z[..., o1:o2], z[..., o2:o3], z[..., o3:]

    lam_bar, b_bar = _s5_discretize(s5_log_dt, s5_a_re, s5_a_im, s5_b_re, s5_b_im)
    y_s5, ssm_new = _s5_scan(u_s5, ssm_h0, lam_bar, b_bar, s5_c_re, s5_c_im, s5_d)
    zz = jax.nn.gelu(y_s5, approximate=False)
    br_s5 = (zz * jax.nn.sigmoid(zz @ s5_glu_w + s5_glu_b)) @ w_s5_proj

    a = a_pre[..., :CONV_WIDTH] * jax.nn.sigmoid(a_pre[..., CONV_WIDTH:])
    c_out, conv_new = _conv_module(a, conv_state, conv_dw_w, conv_dw_b, conv_ln_g, conv_ln_b)
    br_conv = c_out @ w_conv_proj

    o = _cross_attend(q.reshape(bsz, t, XA_HEADS, XA_HEAD_DIM), mem_k, mem_v)
    br_xa = o.reshape(bsz, t, XA_WIDTH) @ w_xa_proj

    g = jax.nn.sigmoid(gl.astype(jnp.float32)).astype(h.dtype).reshape(bsz, t, N_BRANCH, d)
    merged = g[:, :, 0] * br_s5 + g[:, :, 1] * br_conv + g[:, :, 2] * br_xa
    h = h + merged @ w_out

    h = h + _peer(_rmsnorm(h, norm2_g), peer_w_q, peer_sub_keys, peer_u, peer_v)
    return h, ssm_new, conv_new


def setup_inputs(seed: int = 0) -> dict:
    key = jax.random.key(seed)
    ks = list(jax.random.split(key, 48))

    def nrm(shape, scale):
        return scale * jax.random.normal(ks.pop(), shape, jnp.float32)

    L = DEPTH
    G, N = S5_GROUPS, S5_STATE
    n_idx = jnp.arange(N, dtype=jnp.float32)
    return {
        'x_prompt': nrm((BATCH, SEQ, D_MODEL), 1.0),
        'x_sample': nrm((DEC_BATCH, DEC_SEQ, D_MODEL), 1.0),
        'mem_prompt': nrm((BATCH, N_MEM, D_MODEL), 1.0),
        'state_ssm_re': nrm((L, DEC_BATCH, G, N), 0.5),
        'state_ssm_im': nrm((L, DEC_BATCH, G, N), 0.5),
        'state_conv': nrm((L, DEC_BATCH, CONV_K - 1, CONV_WIDTH), 0.5),
        'cache_mem_k': nrm((L, DEC_BATCH, N_MEM, XA_HEADS, XA_HEAD_DIM), 1.0),
        'cache_mem_v': nrm((L, DEC_BATCH, N_MEM, XA_HEADS, XA_HEAD_DIM), 1.0),
        'norm1_g': 1.0 + nrm((L, D_MODEL), 0.02),
        'w_in': nrm((L, D_MODEL, IN_WIDTH), D_MODEL ** -0.5),
        'b_in': nrm((L, IN_WIDTH), 0.02),
        's5_log_dt': jax.random.uniform(ks.pop(), (L, G), jnp.float32,
                                        minval=math.log(1e-3), maxval=math.log(1e-1)),
        's5_a_re': -0.5 * jnp.exp(nrm((L, G, N), 0.05)),
        's5_a_im': math.pi * n_idx + nrm((L, G, N), 0.01),
        's5_b_re': nrm((L, G, N, S5_GROUP), S5_GROUP ** -0.5),
        's5_b_im': nrm((L, G, N, S5_GROUP), S5_GROUP ** -0.5),
        's5_c_re': nrm((L, G, S5_GROUP, N), N ** -0.5),
        's5_c_im': nrm((L, G, S5_GROUP, N), N ** -0.5),
        's5_d': nrm((L, G, S5_GROUP), 1.0),
        's5_glu_w': nrm((L, S5_WIDTH, S5_WIDTH), S5_WIDTH ** -0.5),
        's5_glu_b': nrm((L, S5_WIDTH), 0.02),
        'w_s5_proj': nrm((L, S5_WIDTH, D_MODEL), S5_WIDTH ** -0.5),
        'conv_dw_w': nrm((L, CONV_K, CONV_WIDTH), CONV_K ** -0.5),
        'conv_dw_b': nrm((L, CONV_WIDTH), 0.02),
        'conv_ln_g': 1.0 + nrm((L, CONV_WIDTH), 0.02),
        'conv_ln_b': nrm((L, CONV_WIDTH), 0.02),
        'w_conv_proj': nrm((L, CONV_WIDTH, D_MODEL), CONV_WIDTH ** -0.5),
        'mem_norm_g': 1.0 + nrm((L, D_MODEL), 0.02),
        'w_mem_k': nrm((L, D_MODEL, XA_WIDTH), D_MODEL ** -0.5),
        'w_mem_v': nrm((L, D_MODEL, XA_WIDTH), D_MODEL ** -0.5),
        'w_xa_proj': nrm((L, XA_WIDTH, D_MODEL), XA_WIDTH ** -0.5),
        'w_out': nrm((L, D_MODEL, D_MODEL), D_MODEL ** -0.5),
        'norm2_g': 1.0 + nrm((L, D_MODEL), 0.02),
        'peer_w_q': nrm((L, D_MODEL, PEER_HEADS * PEER_KEY_DIM), D_MODEL ** -0.5),
        'peer_sub_keys': nrm((L, 2, PEER_NKEYS, PEER_HALF), PEER_HALF ** -0.5),
        'peer_u': nrm((L, PEER_N_EXPERTS, D_MODEL), D_MODEL ** -0.5),
        'peer_v': nrm((L, PEER_N_EXPERTS, D_MODEL), 0.3),
        'final_norm_g': 1.0 + nrm((D_MODEL,), 0.02),
    }


def reference(x_prompt, x_sample, mem_prompt, state_ssm_re, state_ssm_im, state_conv,
              cache_mem_k, cache_mem_v, norm1_g, w_in, b_in, s5_log_dt, s5_a_re, s5_a_im,
              s5_b_re, s5_b_im, s5_c_re, s5_c_im, s5_d, s5_glu_w, s5_glu_b, w_s5_proj,
              conv_dw_w, conv_dw_b, conv_ln_g, conv_ln_b, w_conv_proj, mem_norm_g,
              w_mem_k, w_mem_v, w_xa_proj, w_out, norm2_g, peer_w_q, peer_sub_keys,
              peer_u, peer_v, final_norm_g):
    hp, hs = x_prompt, x_sample
    bp = x_prompt.shape[0]
    re_p, im_p, conv_p, mk_p, mv_p, re_s, im_s, conv_s = [], [], [], [], [], [], [], []
    for l in range(DEPTH):
        lw = (norm1_g[l], w_in[l], b_in[l], s5_log_dt[l], s5_a_re[l], s5_a_im[l],
              s5_b_re[l], s5_b_im[l], s5_c_re[l], s5_c_im[l], s5_d[l], s5_glu_w[l],
              s5_glu_b[l], w_s5_proj[l], conv_dw_w[l], conv_dw_b[l], conv_ln_g[l],
              conv_ln_b[l], w_conv_proj[l], w_xa_proj[l], w_out[l], norm2_g[l],
              peer_w_q[l], peer_sub_keys[l], peer_u[l], peer_v[l])
        mk, mv = _mem_kv(mem_prompt, mem_norm_g[l], w_mem_k[l], w_mem_v[l])
        h0p = jnp.zeros((bp, S5_GROUPS, S5_STATE), jnp.complex64)
        cs0p = jnp.zeros((bp, CONV_K - 1, CONV_WIDTH), hp.dtype)
        hp, ssm_p, cnv_p = _trunk_layer(hp, mk, mv, h0p, cs0p, *lw)
        re_p.append(ssm_p.real)
        im_p.append(ssm_p.imag)
        conv_p.append(cnv_p)
        mk_p.append(mk)
        mv_p.append(mv)
        h0s = lax.complex(state_ssm_re[l].astype(jnp.float32), state_ssm_im[l].astype(jnp.float32))
        hs, ssm_s, cnv_s = _trunk_layer(hs, cache_mem_k[l], cache_mem_v[l], h0s, state_conv[l], *lw)
        re_s.append(ssm_s.real)
        im_s.append(ssm_s.imag)
        conv_s.append(cnv_s)
    y_prompt = _rmsnorm(hp, final_norm_g)
    y_sample = _rmsnorm(hs, final_norm_g)
    return (y_prompt, y_sample, jnp.stack(re_p), jnp.stack(im_p), jnp.stack(conv_p),
            jnp.stack(mk_p), jnp.stack(mv_p), jnp.stack(re_s), jnp.stack(im_s), jnp.stack(conv_s))
```

```python
import functools
import math

import jax
import jax.numpy as jnp
from jax import lax
from jax.experimental import pallas as pl
from jax.experimental.pallas import tpu as pltpu

F32 = jnp.float32
BF = jnp.bfloat16

D_MODEL = 1024
WIDTH = 512
S5_GROUPS = 32
S5_GROUP = 16
S5_STATE = 64
S5_CH = S5_GROUPS * S5_STATE
CONV_K = 31
HIST = CONV_K - 1
N_MEM = 256
XA_HEADS = 4
XA_HEAD_DIM = 128
PEER_HEADS = 8
PEER_NKEYS = 128
PEER_N_EXPERTS = PEER_NKEYS * PEER_NKEYS
PEER_TOPK = 16
RMS_EPS = 1e-6
LN_EPS = 1e-5

ROW_TILE = 512
PEER_TOKENS = 512
PEER_EXPERTS = 1024
PEER_ROWS = 16
VMEM_LIMIT = 56 * 1024 * 1024

_NT = (((1,), (1,)), ((), ()))


def _params(sem):
    return pltpu.CompilerParams(dimension_semantics=sem, vmem_limit_bytes=VMEM_LIMIT)


def _rms(x, g):
    return x * lax.rsqrt(jnp.mean(x * x, axis=-1, keepdims=True) + RMS_EPS) * g


def _sigmoid(x):
    return 1.0 / (1.0 + jnp.exp(-x))


def _gelu(x):
    return 0.5 * x * (1.0 + lax.erf(x * (1.0 / math.sqrt(2.0))))


def _const(shape):
    nd = len(shape)
    return pl.BlockSpec(shape, lambda *_: (0,) * nd, pipeline_mode=pl.Buffered(1))


def _memkv_kernel(m_ref, g_ref, wk_ref, wv_ref, k_ref, v_ref):
    m = _rms(m_ref[...], g_ref[...]).astype(BF)
    k_ref[...] = jnp.dot(m, wk_ref[...], preferred_element_type=F32)
    v_ref[...] = jnp.dot(m, wv_ref[...], preferred_element_type=F32)


def _memkv(mem2d, g, wk, wv):
    rows = mem2d.shape[0]
    return pl.pallas_call(
        _memkv_kernel,
        grid=(rows // ROW_TILE,),
        in_specs=[pl.BlockSpec((ROW_TILE, D_MODEL), lambda i: (i, 0)), _const((1, D_MODEL)),
                  _const((D_MODEL, WIDTH)), _const((D_MODEL, WIDTH))],
        out_specs=[pl.BlockSpec((ROW_TILE, WIDTH), lambda i: (i, 0))] * 2,
        out_shape=[jax.ShapeDtypeStruct((rows, WIDTH), F32)] * 2,
        compiler_params=_params(("parallel",)),
        name="mem_kv",
    )(mem2d, g, wk, wv)


def _inproj_kernel(x_ref, g_ref, w_ref, b_ref, u_ref, a_ref, q_ref):
    n = _rms(x_ref[...], g_ref[...]).astype(BF)
    z = jnp.dot(n, w_ref[...], preferred_element_type=F32) + b_ref[...]
    u_ref[...] = z[:, :WIDTH]
    a_ref[...] = z[:, WIDTH:2 * WIDTH] * _sigmoid(z[:, 2 * WIDTH:3 * WIDTH])
    q_ref[...] = z[:, 3 * WIDTH:4 * WIDTH].astype(q_ref.dtype)


def _inproj(x3, g, w, b):
    bn, tn, _ = x3.shape
    tm = min(ROW_TILE, tn)
    tmaj = pl.BlockSpec((tm, WIDTH), lambda bi, i: (i, bi))
    return pl.pallas_call(
        _inproj_kernel,
        grid=(bn, tn // tm),
        in_specs=[pl.BlockSpec((None, tm, D_MODEL), lambda bi, i: (bi, i, 0)), _const((1, D_MODEL)),
                  _const((D_MODEL, 4 * WIDTH)), _const((1, 4 * WIDTH))],
        out_specs=[tmaj, tmaj, pl.BlockSpec((None, tm, WIDTH), lambda bi, i: (bi, i, 0))],
        out_shape=[jax.ShapeDtypeStruct((tn, bn * WIDTH), F32), jax.ShapeDtypeStruct((tn, bn * WIDTH), F32),
                   jax.ShapeDtypeStruct((bn, tn, WIDTH), BF)],
        compiler_params=_params(("parallel", "parallel")),
        name="in_proj",
    )(x3, g, w, b)


def _seq_kernel(nb, tt, cw, n_steps,
                u_ref, a_ref, bmat_ref, cmat_ref, dsk_ref, ldt_ref, are_ref, aim_ref, h0r_ref, h0i_ref,
                hist_ref, dww_ref, dwb_ref, lng_ref, lnb_ref, gluw_ref, glub_ref,
                s5_ref, co_ref, sre_ref, sim_ref,
                bu_scr, st_scr, par_scr, cb_scr):
    i = pl.program_id(0)
    rt = nb * tt

    @pl.when(i == 0)
    def _():
        st_scr[0] = h0r_ref[...]
        st_scr[1] = h0i_ref[...]
        cb_scr[0:HIST * nb, :] = hist_ref[...]

    dt = jnp.exp(ldt_ref[...])
    are = are_ref[...]
    aim = aim_ref[...]
    mag = jnp.exp(are * dt)
    lr = mag * jnp.cos(aim * dt)
    li = mag * jnp.sin(aim * dt)
    den = are * are + aim * aim
    nr = lr - 1.0
    par_scr[0:1, :] = lr
    par_scr[1:2, :] = li
    par_scr[2:3, :] = (nr * are + li * aim) / den
    par_scr[3:4, :] = (li * are - nr * aim) / den

    u = u_ref[...]
    bu_scr[...] = jnp.dot(u.astype(BF), bmat_ref[...], preferred_element_type=F32)

    for c in range(S5_CH // cw):
        re_sl = slice(c * cw, (c + 1) * cw)
        im_sl = slice(S5_CH + c * cw, S5_CH + (c + 1) * cw)
        lr_b = jnp.broadcast_to(par_scr[0:1, re_sl], (nb, cw))
        li_b = jnp.broadcast_to(par_scr[1:2, re_sl], (nb, cw))
        cr_b = jnp.broadcast_to(par_scr[2:3, re_sl], (nb, cw))
        ci_b = jnp.broadcast_to(par_scr[3:4, re_sl], (nb, cw))

        def step(t, carry, re_sl=re_sl, im_sl=im_sl, lr_b=lr_b, li_b=li_b, cr_b=cr_b, ci_b=ci_b):
            s_re, s_im = carry
            r0 = pl.multiple_of(t * nb, nb)
            rr = bu_scr[pl.ds(r0, nb), re_sl]
            ri = bu_scr[pl.ds(r0, nb), im_sl]
            n_re = lr_b * s_re - li_b * s_im + (cr_b * rr - ci_b * ri)
            n_im = lr_b * s_im + li_b * s_re + (cr_b * ri + ci_b * rr)
            bu_scr[pl.ds(r0, nb), re_sl] = n_re
            bu_scr[pl.ds(r0, nb), im_sl] = n_im
            return n_re, n_im

        s_re, s_im = lax.fori_loop(0, tt, step, (st_scr[0, :, re_sl], st_scr[1, :, re_sl]),
                                   unroll=min(tt, 8))
        st_scr[0, :, re_sl] = s_re
        st_scr[1, :, re_sl] = s_im

    y = jnp.dot(bu_scr[...].astype(BF), cmat_ref[...], preferred_element_type=F32) + dsk_ref[...] * u
    zz = _gelu(y)
    gate = _sigmoid(jnp.dot(zz.astype(BF), gluw_ref[...], preferred_element_type=F32) + glub_ref[...])
    s5_ref[...] = (zz * gate).astype(s5_ref.dtype)

    cb_scr[HIST * nb:HIST * nb + rt, :] = a_ref[...]
    rb = 32

    def cblock(bi, carry):
        r0 = pl.multiple_of(bi * rb, rb)
        acc = jnp.zeros((rb, WIDTH), F32)
        for k in range(CONV_K):
            acc = acc + dww_ref[k:k + 1, :] * cb_scr[pl.ds(r0 + k * nb, rb), :]
        yv = acc + dwb_ref[...]
        yc = yv - jnp.mean(yv, axis=-1, keepdims=True)
        var = jnp.mean(yc * yc, axis=-1, keepdims=True)
        ln = yc * lax.rsqrt(var + LN_EPS) * lng_ref[...] + lnb_ref[...]
        co_ref[pl.ds(r0, rb), :] = (ln * _sigmoid(ln)).astype(co_ref.dtype)
        return carry

    lax.fori_loop(0, rt // rb, cblock, 0)
    if n_steps > 1:
        cb_scr[0:HIST * nb, :] = cb_scr[rt:rt + HIST * nb, :]

    @pl.when(i == n_steps - 1)
    def _():
        sre_ref[...] = st_scr[0]
        sim_ref[...] = st_scr[1]


def _seq(u_tm, a_tm, nb, tt, cw, h0r, h0i, hist, sw):
    rows = u_tm.shape[0]
    rt = nb * tt
    n_steps = rows // rt
    tile = pl.BlockSpec((rt, WIDTH), lambda i: (i, 0))
    st = _const((nb, S5_CH))
    vec = _const((1, WIDTH))
    chv = _const((1, S5_CH))
    return pl.pallas_call(
        functools.partial(_seq_kernel, nb, tt, cw, n_steps),
        grid=(n_steps,),
        in_specs=[tile, tile, _const((WIDTH, 2 * S5_CH)), _const((2 * S5_CH, WIDTH)), vec, chv, chv, chv, st, st,
                  _const((HIST * nb, WIDTH)), _const((CONV_K, WIDTH)), vec, vec, vec, _const((WIDTH, WIDTH)), vec],
        out_specs=[tile, tile, st, st],
        out_shape=[jax.ShapeDtypeStruct((rows, WIDTH), BF), jax.ShapeDtypeStruct((rows, WIDTH), BF),
                   jax.ShapeDtypeStruct((nb, S5_CH), F32), jax.ShapeDtypeStruct((nb, S5_CH), F32)],
        scratch_shapes=[pltpu.VMEM((rt, 2 * S5_CH), F32), pltpu.VMEM((2, nb, S5_CH), F32),
                        pltpu.VMEM((8, S5_CH), F32), pltpu.VMEM(((HIST + tt) * nb, WIDTH), F32)],
        compiler_params=_params(("arbitrary",)),
        name="s5_conv",
    )(u_tm, a_tm, sw["bmat"], sw["cmat"], sw["dskip"], sw["ldt"], sw["are"], sw["aim"], h0r, h0i, hist,
      sw["dww"], sw["dwb"], sw["lng"], sw["lnb"], sw["gluw"], sw["glub"])


def _attn_heads(q, k, v):
    scale = XA_HEAD_DIM ** -0.5
    outs = []
    for h in range(XA_HEADS):
        sl = slice(h * XA_HEAD_DIM, (h + 1) * XA_HEAD_DIM)
        s = lax.dot_general(q[:, sl], k[:, sl], _NT, preferred_element_type=F32) * scale
        e = jnp.exp(s - jnp.max(s, axis=-1, keepdims=True))
        p = e / jnp.sum(e, axis=-1, keepdims=True)
        outs.append(jnp.dot(p.astype(BF), v[:, sl], preferred_element_type=F32))
    return jnp.concatenate(outs, axis=-1)


def _attn_prompt_kernel(q_ref, k_ref, v_ref, o_ref):
    o_ref[...] = _attn_heads(q_ref[...], k_ref[...].astype(BF), v_ref[...].astype(BF)).astype(o_ref.dtype)


def _attn_prompt(q3, k3, v3):
    bn, tn, _ = q3.shape
    tm = min(ROW_TILE, tn)
    kv = pl.BlockSpec((None, N_MEM, WIDTH), lambda bi, i: (bi, 0, 0))
    qs = pl.BlockSpec((None, tm, WIDTH), lambda bi, i: (bi, i, 0))
    return pl.pallas_call(
        _attn_prompt_kernel,
        grid=(bn, tn // tm),
        in_specs=[qs, kv, kv],
        out_specs=qs,
        out_shape=jax.ShapeDtypeStruct((bn, tn, WIDTH), BF),
        compiler_params=_params(("parallel", "parallel")),
        name="xattn_prompt",
    )(q3, k3, v3)


def _attn_sample_kernel(bb, tq, q_ref, k_ref, v_ref, o_ref):
    nr = XA_HEADS * tq
    row_head = lax.broadcasted_iota(jnp.int32, (nr, WIDTH), 0) // tq
    lane_head = lax.broadcasted_iota(jnp.int32, (nr, WIDTH), 1) // XA_HEAD_DIM
    own = row_head == lane_head
    scale = XA_HEAD_DIM ** -0.5
    for b in range(bb):
        qb = q_ref[b]
        qrows = jnp.where(own, jnp.concatenate([qb] * XA_HEADS, axis=0), 0.0).astype(BF)
        s = lax.dot_general(qrows, k_ref[b].astype(BF), _NT, preferred_element_type=F32) * scale
        e = jnp.exp(s - jnp.max(s, axis=-1, keepdims=True))
        p = e / jnp.sum(e, axis=-1, keepdims=True)
        of = jnp.where(own, jnp.dot(p.astype(BF), v_ref[b].astype(BF), preferred_element_type=F32), 0.0)
        ob = of[0:tq]
        for h in range(1, XA_HEADS):
            ob = ob + of[h * tq:(h + 1) * tq]
        o_ref[b] = ob


def _attn_sample(q3, k3, v3):
    bn, tq, _ = q3.shape
    bb = 8
    kv = pl.BlockSpec((bb, N_MEM, WIDTH), lambda i: (i, 0, 0))
    qs = pl.BlockSpec((bb, tq, WIDTH), lambda i: (i, 0, 0))
    return pl.pallas_call(
        functools.partial(_attn_sample_kernel, bb, tq),
        grid=(bn // bb,),
        in_specs=[qs, kv, kv],
        out_specs=qs,
        out_shape=jax.ShapeDtypeStruct((bn, tq, WIDTH), F32),
        compiler_params=_params(("parallel",)),
        name="xattn_sample",
    )(q3, k3, v3)


def _merge_kernel(x_ref, s5_ref, cv_ref, xa_ref, g_ref, wg_ref, bg_ref, ws5_ref, wcv_ref, wxa_ref, wout_ref, h_ref):
    x = x_ref[...]
    n = _rms(x, g_ref[...]).astype(BF)
    gate = _sigmoid(jnp.dot(n, wg_ref[...], preferred_element_type=F32) + bg_ref[...])
    merged = (gate[:, :D_MODEL] * jnp.dot(s5_ref[...], ws5_ref[...], preferred_element_type=F32)
              + gate[:, D_MODEL:2 * D_MODEL] * jnp.dot(cv_ref[...], wcv_ref[...], preferred_element_type=F32)
              + gate[:, 2 * D_MODEL:] * jnp.dot(xa_ref[...], wxa_ref[...], preferred_element_type=F32))
    h_ref[...] = x + jnp.dot(merged.astype(BF), wout_ref[...], preferred_element_type=F32)


def _merge(x3, s5_tm2, cv_tm2, xa3, g, wg, bg, ws5, wcv, wxa, wout):
    bn, tn, _ = x3.shape
    tm = min(ROW_TILE, tn)
    xs = pl.BlockSpec((None, tm, D_MODEL), lambda bi, i: (bi, i, 0))
    tmaj = pl.BlockSpec((tm, WIDTH), lambda bi, i: (i, bi))
    proj = _const((WIDTH, D_MODEL))
    return pl.pallas_call(
        _merge_kernel,
        grid=(bn, tn // tm),
        in_specs=[xs, tmaj, tmaj, pl.BlockSpec((None, tm, WIDTH), lambda bi, i: (bi, i, 0)), _const((1, D_MODEL)),
                  _const((D_MODEL, 3 * D_MODEL)), _const((1, 3 * D_MODEL)), proj, proj, proj,
                  _const((D_MODEL, D_MODEL))],
        out_specs=xs,
        out_shape=jax.ShapeDtypeStruct((bn, tn, D_MODEL), F32),
        compiler_params=_params(("parallel", "parallel")),
        name="merge_out",
    )(x3, s5_tm2, cv_tm2, xa3, g, wg, bg, ws5, wcv, wxa, wout)


def _top_values(work, count):
    vals = []
    for _ in range(count):
        m = jnp.max(work, axis=0, keepdims=True)
        vals.append(m)
        work = jnp.where(work == m, -jnp.inf, work)
    return vals


def _peer_kernel(n_esteps, h_ref, g2_ref, wqt_ref, keys_ref, u_ref, vt_ref, gf_ref, y_ref,
                 x2_scr, st_scr, sv_scr, e1_scr, e2_scr, cz_scr, at_scr, mt_scr, acc_scr):
    j = pl.program_id(1)
    tn = PEER_TOKENS
    nk = PEER_TOPK + 1
    lanes = 128

    @pl.when(j == 0)
    def _():
        x2 = _rms(h_ref[...], g2_ref[...]).astype(BF)
        x2_scr[...] = x2
        qt = lax.dot_general(wqt_ref[...], x2, _NT, preferred_element_type=F32)
        for hp in range(2 * PEER_HEADS):
            st_scr[hp] = jnp.dot(keys_ref[hp % 2], qt[hp * PEER_NKEYS:(hp + 1) * PEER_NKEYS].astype(BF),
                                 preferred_element_type=F32)

        def half_body(hp, carry):
            for lt in range(tn // lanes):
                ls = slice(lt * lanes, (lt + 1) * lanes)
                vals = _top_values(st_scr[hp, :, ls], nk)
                for k in range(nk):
                    sv_scr[hp, k:k + 1, ls] = vals[k]
                sv_scr[hp, nk:24, ls] = jnp.full((24 - nk, lanes), -jnp.inf, F32)
            return carry

        lax.fori_loop(0, 2 * PEER_HEADS, half_body, 0)

        row8 = lax.broadcasted_iota(jnp.int32, (8, lanes), 0)

        def head_body(h, carry):
            for lt in range(tn // lanes):
                ls = slice(lt * lanes, (lt + 1) * lanes)
                sv1 = sv_scr[2 * h, :, ls]
                sv2 = sv_scr[2 * h + 1, :, ls]
                cands = [sv1[0:1] + sv2]
                for r in range(2, nk + 1):
                    cands.append(jnp.where(row8 < nk // r, sv1[r - 1:r] + sv2[0:8], -jnp.inf))
                cv = _top_values(jnp.concatenate(cands, axis=0), nk)
                z = jnp.ones_like(cv[0])
                for k in range(1, PEER_TOPK):
                    z = z + jnp.exp(cv[k] - cv[0])
                tau = 0.5 * (cv[PEER_TOPK - 1] + cv[PEER_TOPK])
                cz_scr[h, :, ls] = jnp.broadcast_to(jnp.exp(tau - cv[0]) / z, (8, lanes))
                e1_scr[h, :, ls] = jnp.exp(st_scr[2 * h, :, ls] - sv1[0:1]) / z
                e2_scr[h, :, ls] = jnp.exp(st_scr[2 * h + 1, :, ls] - sv2[0:1])
            return carry

        lax.fori_loop(0, PEER_HEADS, head_body, 0)
        acc_scr[...] = jnp.zeros_like(acc_scr)

    at_scr[...] = lax.dot_general(u_ref[...], x2_scr[...], _NT, preferred_element_type=F32)

    def rows_body(rb, carry):
        r0 = pl.multiple_of(rb * PEER_ROWS, PEER_ROWS)
        e0 = j * PEER_EXPERTS + r0
        i1 = lax.shift_right_logical(e0, 7)
        i2 = pl.multiple_of(lax.bitwise_and(e0, PEER_NKEYS - 1), PEER_ROWS)
        act = _gelu(at_scr[pl.ds(r0, PEER_ROWS), :])
        w = jnp.zeros((PEER_ROWS, tn), F32)
        for h in range(PEER_HEADS):
            p = e1_scr[h, pl.ds(i1, 1), :] * e2_scr[h, pl.ds(i2, PEER_ROWS), :]
            w = w + jnp.where(p >= cz_scr[h, 0:1, :], p, 0.0)
        mt_scr[pl.ds(r0, PEER_ROWS), :] = (w * act).astype(BF)
        return carry

    lax.fori_loop(0, PEER_EXPERTS // PEER_ROWS, rows_body, 0)
    acc_scr[...] += jnp.dot(vt_ref[...], mt_scr[...], preferred_element_type=F32)

    @pl.when(j == n_esteps - 1)
    def _():
        h2 = h_ref[...] + acc_scr[...].T
        y_ref[...] = _rms(h2, gf_ref[...])


def _peer_final(h2d, g2, wqt, keys, u_bf, vt_bf, gf):
    rows = h2d.shape[0]
    tn = PEER_TOKENS
    n_esteps = PEER_N_EXPERTS // PEER_EXPERTS
    hs = pl.BlockSpec((tn, D_MODEL), lambda i, j: (i, 0))
    return pl.pallas_call(
        functools.partial(_peer_kernel, n_esteps),
        grid=(rows // tn, n_esteps),
        in_specs=[hs, _const((1, D_MODEL)), _const((2 * PEER_HEADS * PEER_NKEYS, D_MODEL)),
                  _const((2, PEER_NKEYS, PEER_NKEYS)),
                  pl.BlockSpec((PEER_EXPERTS, D_MODEL), lambda i, j: (j, 0)),
                  pl.BlockSpec((D_MODEL, PEER_EXPERTS), lambda i, j: (0, j)), _const((1, D_MODEL))],
        out_specs=hs,
        out_shape=jax.ShapeDtypeStruct((rows, D_MODEL), F32),
        scratch_shapes=[pltpu.VMEM((tn, D_MODEL), BF),
                        pltpu.VMEM((2 * PEER_HEADS, PEER_NKEYS, tn), F32),
                        pltpu.VMEM((2 * PEER_HEADS, 24, tn), F32),
                        pltpu.VMEM((PEER_HEADS, PEER_NKEYS, tn), F32),
                        pltpu.VMEM((PEER_HEADS, PEER_NKEYS, tn), F32),
                        pltpu.VMEM((PEER_HEADS, 8, tn), F32),
                        pltpu.VMEM((PEER_EXPERTS, tn), F32),
                        pltpu.VMEM((PEER_EXPERTS, tn), BF),
                        pltpu.VMEM((D_MODEL, tn), F32)],
        compiler_params=_params(("parallel", "arbitrary")),
        name="peer_final",
    )(h2d, g2, wqt, keys, u_bf, vt_bf, gf)


def _block_diag(w):
    g, r, c = w.shape
    idx = jnp.arange(g)
    return jnp.zeros((g, r, g, c), w.dtype).at[idx, :, idx, :].set(w).reshape(g * r, g * c)


def _row(v):
    return v.reshape(1, -1).astype(F32)


def kernel(x_prompt, x_sample, mem_prompt, state_ssm_re, state_ssm_im, state_conv, cache_mem_k, cache_mem_v,
           norm1_g, w_in, b_in, s5_log_dt, s5_a_re, s5_a_im, s5_b_re, s5_b_im, s5_c_re, s5_c_im, s5_d,
           s5_glu_w, s5_glu_b, w_s5_proj, conv_dw_w, conv_dw_b, conv_ln_g, conv_ln_b, w_conv_proj, mem_norm_g,
           w_mem_k, w_mem_v, w_xa_proj, w_out, norm2_g, peer_w_q, peer_sub_keys, peer_u, peer_v, final_norm_g):
    assert norm1_g.shape[0] == 1, "single-layer trunk"
    bp, tp, _ = x_prompt.shape
    bs, ts, _ = x_sample.shape
    l = 0

    w1 = w_in[l][:, :4 * WIDTH].astype(BF)
    b1 = _row(b_in[l][:4 * WIDTH])
    wg = w_in[l][:, 4 * WIDTH:].astype(BF)
    bg = _row(b_in[l][4 * WIDTH:])
    sw = {
        "bmat": jnp.concatenate([_block_diag(jnp.swapaxes(s5_b_re[l], 1, 2)),
                                 _block_diag(jnp.swapaxes(s5_b_im[l], 1, 2))], axis=1).astype(BF),
        "cmat": jnp.concatenate([_block_diag(jnp.swapaxes(s5_c_re[l], 1, 2)),
                                 -_block_diag(jnp.swapaxes(s5_c_im[l], 1, 2))], axis=0).astype(BF),
        "dskip": _row(s5_d[l]),
        "ldt": _row(jnp.repeat(s5_log_dt[l], S5_STATE)),
        "are": _row(s5_a_re[l]),
        "aim": _row(s5_a_im[l]),
        "dww": conv_dw_w[l].astype(F32),
        "dwb": _row(conv_dw_b[l]),
        "lng": _row(conv_ln_g[l]),
        "lnb": _row(conv_ln_b[l]),
        "gluw": s5_glu_w[l].astype(BF),
        "glub": _row(s5_glu_b[l]),
    }
    g1 = _row(norm1_g[l])
    merge_w = (g1, wg, bg, w_s5_proj[l].astype(BF), w_conv_proj[l].astype(BF), w_xa_proj[l].astype(BF),
               w_out[l].astype(BF))
    peer_w = (_row(norm2_g[l]), peer_w_q[l].T.astype(BF), peer_sub_keys[l].astype(BF), peer_u[l].astype(BF),
              peer_v[l].T.astype(BF), _row(final_norm_g))

    mk, mv = _memkv(mem_prompt.reshape(bp * N_MEM, D_MODEL), _row(mem_norm_g[l]),
                    w_mem_k[l].astype(BF), w_mem_v[l].astype(BF))
    u2, a2, q3 = _inproj(x_prompt, g1, w1, b1)
    zeros_st = jnp.zeros((bp, S5_CH), F32)
    s5_tm, cv_tm, re_p, im_p = _seq(u2.reshape(tp * bp, WIDTH), a2.reshape(tp * bp, WIDTH), bp, 64, 512,
                                    zeros_st, zeros_st, jnp.zeros((HIST * bp, WIDTH), F32), sw)
    xa3 = _attn_prompt(q3, mk.reshape(bp, N_MEM, WIDTH), mv.reshape(bp, N_MEM, WIDTH))
    h_p = _merge(x_prompt, s5_tm.reshape(tp, bp * WIDTH), cv_tm.reshape(tp, bp * WIDTH), xa3, *merge_w)
    y_prompt = _peer_final(h_p.reshape(bp * tp, D_MODEL), *peer_w).reshape(bp, tp, D_MODEL)
    conv_p = jnp.swapaxes(a2[tp - HIST:].reshape(HIST, bp, WIDTH), 0, 1)

    n_s = bs * ts
    us, as_, qs = _inproj(x_sample.reshape(1, n_s, D_MODEL), g1, w1, b1)

    def to_tm(v):
        return jnp.swapaxes(v.reshape(bs, ts, WIDTH), 0, 1).reshape(n_s, WIDTH)

    def to_bm(v):
        return jnp.swapaxes(v.reshape(ts, bs, WIDTH), 0, 1).reshape(n_s, WIDTH)

    hist_s = jnp.swapaxes(state_conv[l], 0, 1).reshape(HIST * bs, WIDTH)
    s5_s, cv_s, re_s, im_s = _seq(to_tm(us), to_tm(as_), bs, ts, 128,
                                  state_ssm_re[l].reshape(bs, S5_CH), state_ssm_im[l].reshape(bs, S5_CH), hist_s, sw)
    xa_s = _attn_sample(qs.reshape(bs, ts, WIDTH).astype(F32), cache_mem_k[l].reshape(bs, N_MEM, WIDTH),
                        cache_mem_v[l].reshape(bs, N_MEM, WIDTH))
    h_s = _merge(x_sample.reshape(1, n_s, D_MODEL), to_bm(s5_s), to_bm(cv_s),
                 xa_s.reshape(1, n_s, WIDTH).astype(BF), *merge_w)
    y_sample = _peer_final(h_s.reshape(n_s, D_MODEL), *peer_w).reshape(bs, ts, D_MODEL)
    conv_s = jnp.concatenate([state_conv[l][:, ts:], as_.reshape(bs, ts, WIDTH)], axis=1)

    st_shape_p = (1, bp, S5_GROUPS, S5_STATE)
    st_shape_s = (1, bs, S5_GROUPS, S5_STATE)
    return (y_prompt, y_sample, re_p.reshape(st_shape_p), im_p.reshape(st_shape_p), conv_p[None],
            mk.reshape(1, bp, N_MEM, XA_HEADS, XA_HEAD_DIM), mv.reshape(1, bp, N_MEM, XA_HEADS, XA_HEAD_DIM),
            re_s.reshape(st_shape_s), im_s.reshape(st_shape_s), conv_s[None])
```

```python
import functools
import math

import jax
import jax.numpy as jnp
from jax import lax
from jax.experimental import pallas as pl
from jax.experimental.pallas import tpu as pltpu

F32 = jnp.float32
BF = jnp.bfloat16

D_MODEL = 1024
WIDTH = 512
S5_GROUPS = 32
S5_GROUP = 16
S5_STATE = 64
S5_CH = S5_GROUPS * S5_STATE
CONV_K = 31
HIST = CONV_K - 1
N_MEM = 256
XA_HEADS = 4
XA_HEAD_DIM = 128
PEER_HEADS = 8
PEER_NKEYS = 128
PEER_N_EXPERTS = PEER_NKEYS * PEER_NKEYS
PEER_TOPK = 16
RMS_EPS = 1e-6
LN_EPS = 1e-5

ROW_TILE = 512
PEER_TOKENS = 512
PEER_EXPERTS = 2048
PEER_CHUNK = 256
PEER_ROWS = 16
PEER_LANES = 256
PEER_VROWS = 256
VMEM_LIMIT = 56 * 1024 * 1024

_NT = (((1,), (1,)), ((), ()))


def _params(sem):
    return pltpu.CompilerParams(dimension_semantics=sem, vmem_limit_bytes=VMEM_LIMIT)


def _rms(x, g):
    return x * lax.rsqrt(jnp.mean(x * x, axis=-1, keepdims=True) + RMS_EPS) * g


def _sigmoid(x):
    return 1.0 / (1.0 + jnp.exp(-x))


def _gelu(x):
    return 0.5 * x * (1.0 + lax.erf(x * (1.0 / math.sqrt(2.0))))


def _const(shape):
    nd = len(shape)
    return pl.BlockSpec(shape, lambda *_: (0,) * nd, pipeline_mode=pl.Buffered(1))


def _memkv_kernel(m_ref, g_ref, wk_ref, wv_ref, k_ref, v_ref):
    m = _rms(m_ref[...], g_ref[...]).astype(BF)
    k_ref[...] = jnp.dot(m, wk_ref[...], preferred_element_type=F32)
    v_ref[...] = jnp.dot(m, wv_ref[...], preferred_element_type=F32)


def _memkv(mem2d, g, wk, wv):
    rows = mem2d.shape[0]
    return pl.pallas_call(
        _memkv_kernel,
        grid=(rows // ROW_TILE,),
        in_specs=[pl.BlockSpec((ROW_TILE, D_MODEL), lambda i: (i, 0)), _const((1, D_MODEL)),
                  _const((D_MODEL, WIDTH)), _const((D_MODEL, WIDTH))],
        out_specs=[pl.BlockSpec((ROW_TILE, WIDTH), lambda i: (i, 0))] * 2,
        out_shape=[jax.ShapeDtypeStruct((rows, WIDTH), F32)] * 2,
        compiler_params=_params(("parallel",)),
        name="mem_kv",
    )(mem2d, g, wk, wv)


def _inproj_kernel(x_ref, g_ref, w_ref, b_ref, u_ref, a_ref, q_ref):
    n = _rms(x_ref[...], g_ref[...]).astype(BF)
    z = jnp.dot(n, w_ref[...], preferred_element_type=F32) + b_ref[...]
    u_ref[...] = z[:, :WIDTH]
    a_ref[...] = z[:, WIDTH:2 * WIDTH] * _sigmoid(z[:, 2 * WIDTH:3 * WIDTH])
    q_ref[...] = z[:, 3 * WIDTH:4 * WIDTH].astype(q_ref.dtype)


def _inproj(x3, g, w, b):
    bn, tn, _ = x3.shape
    tm = min(ROW_TILE, tn)
    tmaj = pl.BlockSpec((tm, WIDTH), lambda bi, i: (i, bi))
    return pl.pallas_call(
        _inproj_kernel,
        grid=(bn, tn // tm),
        in_specs=[pl.BlockSpec((None, tm, D_MODEL), lambda bi, i: (bi, i, 0)), _const((1, D_MODEL)),
                  _const((D_MODEL, 4 * WIDTH)), _const((1, 4 * WIDTH))],
        out_specs=[tmaj, tmaj, pl.BlockSpec((None, tm, WIDTH), lambda bi, i: (bi, i, 0))],
        out_shape=[jax.ShapeDtypeStruct((tn, bn * WIDTH), F32), jax.ShapeDtypeStruct((tn, bn * WIDTH), F32),
                   jax.ShapeDtypeStruct((bn, tn, WIDTH), BF)],
        compiler_params=_params(("parallel", "parallel")),
        name="in_proj",
    )(x3, g, w, b)


def _seq_kernel(nb, tt, cw, n_steps,
                u_ref, a_ref, bmat_ref, cmat_ref, dsk_ref, ldt_ref, are_ref, aim_ref, h0r_ref, h0i_ref,
                hist_ref, dww_ref, dwb_ref, lng_ref, lnb_ref, gluw_ref, glub_ref,
                s5_ref, co_ref, sre_ref, sim_ref,
                bu_scr, st_scr, par_scr, cb_scr):
    i = pl.program_id(0)
    rt = nb * tt

    @pl.when(i == 0)
    def _():
        st_scr[0] = h0r_ref[...]
        st_scr[1] = h0i_ref[...]
        cb_scr[0:HIST * nb, :] = hist_ref[...]

    dt = jnp.exp(ldt_ref[...])
    are = are_ref[...]
    aim = aim_ref[...]
    mag = jnp.exp(are * dt)
    lr = mag * jnp.cos(aim * dt)
    li = mag * jnp.sin(aim * dt)
    den = are * are + aim * aim
    nr = lr - 1.0
    par_scr[0:1, :] = lr
    par_scr[1:2, :] = li
    par_scr[2:3, :] = (nr * are + li * aim) / den
    par_scr[3:4, :] = (li * are - nr * aim) / den

    u = u_ref[...]
    bu_scr[...] = jnp.dot(u.astype(BF), bmat_ref[...], preferred_element_type=F32)

    for c in range(S5_CH // cw):
        re_sl = slice(c * cw, (c + 1) * cw)
        im_sl = slice(S5_CH + c * cw, S5_CH + (c + 1) * cw)
        lr_b = jnp.broadcast_to(par_scr[0:1, re_sl], (nb, cw))
        li_b = jnp.broadcast_to(par_scr[1:2, re_sl], (nb, cw))
        cr_b = jnp.broadcast_to(par_scr[2:3, re_sl], (nb, cw))
        ci_b = jnp.broadcast_to(par_scr[3:4, re_sl], (nb, cw))

        def step(t, carry, re_sl=re_sl, im_sl=im_sl, lr_b=lr_b, li_b=li_b, cr_b=cr_b, ci_b=ci_b):
            s_re, s_im = carry
            r0 = pl.multiple_of(t * nb, nb)
            rr = bu_scr[pl.ds(r0, nb), re_sl]
            ri = bu_scr[pl.ds(r0, nb), im_sl]
            n_re = lr_b * s_re - li_b * s_im + (cr_b * rr - ci_b * ri)
            n_im = lr_b * s_im + li_b * s_re + (cr_b * ri + ci_b * rr)
            bu_scr[pl.ds(r0, nb), re_sl] = n_re
            bu_scr[pl.ds(r0, nb), im_sl] = n_im
            return n_re, n_im

        s_re, s_im = lax.fori_loop(0, tt, step, (st_scr[0, :, re_sl], st_scr[1, :, re_sl]),
                                   unroll=min(tt, 8))
        st_scr[0, :, re_sl] = s_re
        st_scr[1, :, re_sl] = s_im

    y = jnp.dot(bu_scr[...].astype(BF), cmat_ref[...], preferred_element_type=F32) + dsk_ref[...] * u
    zz = _gelu(y)
    gate = _sigmoid(jnp.dot(zz.astype(BF), gluw_ref[...], preferred_element_type=F32) + glub_ref[...])
    s5_ref[...] = (zz * gate).astype(s5_ref.dtype)

    cb_scr[HIST * nb:HIST * nb + rt, :] = a_ref[...]
    rb = 32

    def cblock(bi, carry):
        r0 = pl.multiple_of(bi * rb, rb)
        acc = jnp.zeros((rb, WIDTH), F32)
        for k in range(CONV_K):
            acc = acc + dww_ref[k:k + 1, :] * cb_scr[pl.ds(r0 + k * nb, rb), :]
        yv = acc + dwb_ref[...]
        yc = yv - jnp.mean(yv, axis=-1, keepdims=True)
        var = jnp.mean(yc * yc, axis=-1, keepdims=True)
        ln = yc * lax.rsqrt(var + LN_EPS) * lng_ref[...] + lnb_ref[...]
        co_ref[pl.ds(r0, rb), :] = (ln * _sigmoid(ln)).astype(co_ref.dtype)
        return carry

    lax.fori_loop(0, rt // rb, cblock, 0)
    if n_steps > 1:
        cb_scr[0:HIST * nb, :] = cb_scr[rt:rt + HIST * nb, :]

    @pl.when(i == n_steps - 1)
    def _():
        sre_ref[...] = st_scr[0]
        sim_ref[...] = st_scr[1]


def _seq(u_tm, a_tm, nb, tt, cw, h0r, h0i, hist, sw):
    rows = u_tm.shape[0]
    rt = nb * tt
    n_steps = rows // rt
    tile = pl.BlockSpec((rt, WIDTH), lambda i: (i, 0))
    st = _const((nb, S5_CH))
    vec = _const((1, WIDTH))
    chv = _const((1, S5_CH))
    return pl.pallas_call(
        functools.partial(_seq_kernel, nb, tt, cw, n_steps),
        grid=(n_steps,),
        in_specs=[tile, tile, _const((WIDTH, 2 * S5_CH)), _const((2 * S5_CH, WIDTH)), vec, chv, chv, chv, st, st,
                  _const((HIST * nb, WIDTH)), _const((CONV_K, WIDTH)), vec, vec, vec, _const((WIDTH, WIDTH)), vec],
        out_specs=[tile, tile, st, st],
        out_shape=[jax.ShapeDtypeStruct((rows, WIDTH), BF), jax.ShapeDtypeStruct((rows, WIDTH), BF),
                   jax.ShapeDtypeStruct((nb, S5_CH), F32), jax.ShapeDtypeStruct((nb, S5_CH), F32)],
        scratch_shapes=[pltpu.VMEM((rt, 2 * S5_CH), F32), pltpu.VMEM((2, nb, S5_CH), F32),
                        pltpu.VMEM((8, S5_CH), F32), pltpu.VMEM(((HIST + tt) * nb, WIDTH), F32)],
        compiler_params=_params(("arbitrary",)),
        name="s5_conv",
    )(u_tm, a_tm, sw["bmat"], sw["cmat"], sw["dskip"], sw["ldt"], sw["are"], sw["aim"], h0r, h0i, hist,
      sw["dww"], sw["dwb"], sw["lng"], sw["lnb"], sw["gluw"], sw["glub"])


def _attn_heads(q, k, v):
    scale = XA_HEAD_DIM ** -0.5
    outs = []
    for h in range(XA_HEADS):
        sl = slice(h * XA_HEAD_DIM, (h + 1) * XA_HEAD_DIM)
        s = lax.dot_general(q[:, sl], k[:, sl], _NT, preferred_element_type=F32) * scale
        e = jnp.exp(s - jnp.max(s, axis=-1, keepdims=True))
        p = e / jnp.sum(e, axis=-1, keepdims=True)
        outs.append(jnp.dot(p.astype(BF), v[:, sl], preferred_element_type=F32))
    return jnp.concatenate(outs, axis=-1)


def _attn_prompt_kernel(q_ref, k_ref, v_ref, o_ref):
    o_ref[...] = _attn_heads(q_ref[...], k_ref[...].astype(BF), v_ref[...].astype(BF)).astype(o_ref.dtype)


def _attn_prompt(q3, k3, v3):
    bn, tn, _ = q3.shape
    tm = min(ROW_TILE, tn)
    kv = pl.BlockSpec((None, N_MEM, WIDTH), lambda bi, i: (bi, 0, 0))
    qs = pl.BlockSpec((None, tm, WIDTH), lambda bi, i: (bi, i, 0))
    return pl.pallas_call(
        _attn_prompt_kernel,
        grid=(bn, tn // tm),
        in_specs=[qs, kv, kv],
        out_specs=qs,
        out_shape=jax.ShapeDtypeStruct((bn, tn, WIDTH), BF),
        compiler_params=_params(("parallel", "parallel")),
        name="xattn_prompt",
    )(q3, k3, v3)


def _attn_sample_kernel(bb, tq, q_ref, k_ref, v_ref, o_ref):
    nr = XA_HEADS * tq
    row_head = lax.broadcasted_iota(jnp.int32, (nr, WIDTH), 0) // tq
    lane_head = lax.broadcasted_iota(jnp.int32, (nr, WIDTH), 1) // XA_HEAD_DIM
    own = row_head == lane_head
    scale = XA_HEAD_DIM ** -0.5
    for b in range(bb):
        qb = q_ref[b]
        qrows = jnp.where(own, jnp.concatenate([qb] * XA_HEADS, axis=0), 0.0).astype(BF)
        s = lax.dot_general(qrows, k_ref[b].astype(BF), _NT, preferred_element_type=F32) * scale
        e = jnp.exp(s - jnp.max(s, axis=-1, keepdims=True))
        p = e / jnp.sum(e, axis=-1, keepdims=True)
        of = jnp.where(own, jnp.dot(p.astype(BF), v_ref[b].astype(BF), preferred_element_type=F32), 0.0)
        ob = of[0:tq]
        for h in range(1, XA_HEADS):
            ob = ob + of[h * tq:(h + 1) * tq]
        o_ref[b] = ob


def _attn_sample(q3, k3, v3):
    bn, tq, _ = q3.shape
    bb = 8
    kv = pl.BlockSpec((bb, N_MEM, WIDTH), lambda i: (i, 0, 0))
    qs = pl.BlockSpec((bb, tq, WIDTH), lambda i: (i, 0, 0))
    return pl.pallas_call(
        functools.partial(_attn_sample_kernel, bb, tq),
        grid=(bn // bb,),
        in_specs=[qs, kv, kv],
        out_specs=qs,
        out_shape=jax.ShapeDtypeStruct((bn, tq, WIDTH), F32),
        compiler_params=_params(("parallel",)),
        name="xattn_sample",
    )(q3, k3, v3)


def _merge_kernel(x_ref, s5_ref, cv_ref, xa_ref, g_ref, wg_ref, bg_ref, ws5_ref, wcv_ref, wxa_ref, wout_ref, h_ref):
    x = x_ref[...]
    n = _rms(x, g_ref[...]).astype(BF)
    gate = _sigmoid(jnp.dot(n, wg_ref[...], preferred_element_type=F32) + bg_ref[...])
    merged = (gate[:, :D_MODEL] * jnp.dot(s5_ref[...], ws5_ref[...], preferred_element_type=F32)
              + gate[:, D_MODEL:2 * D_MODEL] * jnp.dot(cv_ref[...], wcv_ref[...], preferred_element_type=F32)
              + gate[:, 2 * D_MODEL:] * jnp.dot(xa_ref[...], wxa_ref[...], preferred_element_type=F32))
    h_ref[...] = x + jnp.dot(merged.astype(BF), wout_ref[...], preferred_element_type=F32)


def _merge(x3, s5_tm2, cv_tm2, xa3, g, wg, bg, ws5, wcv, wxa, wout):
    bn, tn, _ = x3.shape
    tm = min(ROW_TILE, tn)
    xs = pl.BlockSpec((None, tm, D_MODEL), lambda bi, i: (bi, i, 0))
    tmaj = pl.BlockSpec((tm, WIDTH), lambda bi, i: (i, bi))
    proj = _const((WIDTH, D_MODEL))
    return pl.pallas_call(
        _merge_kernel,
        grid=(bn, tn // tm),
        in_specs=[xs, tmaj, tmaj, pl.BlockSpec((None, tm, WIDTH), lambda bi, i: (bi, i, 0)), _const((1, D_MODEL)),
                  _const((D_MODEL, 3 * D_MODEL)), _const((1, 3 * D_MODEL)), proj, proj, proj,
                  _const((D_MODEL, D_MODEL))],
        out_specs=xs,
        out_shape=jax.ShapeDtypeStruct((bn, tn, D_MODEL), F32),
        compiler_params=_params(("parallel", "parallel")),
        name="merge_out",
    )(x3, s5_tm2, cv_tm2, xa3, g, wg, bg, ws5, wcv, wxa, wout)


def _top_values(work, count):
    vals = []
    for _ in range(count):
        m = jnp.max(work, axis=0, keepdims=True)
        vals.append(m)
        work = jnp.where(work == m, -jnp.inf, work)
    return vals


def _peer_kernel(n_esteps, h_ref, g2_ref, wqt_ref, keys_ref, u_ref, vt_ref, gf_ref, y_ref,
                 x2_scr, st_scr, sv_scr, e1_scr, e2_scr, cz_scr, at_scr, mt_scr, acc_scr):
    j = pl.program_id(1)
    tn = PEER_TOKENS
    nk = PEER_TOPK + 1
    lanes = 128

    @pl.when(j == 0)
    def _():
        x2 = _rms(h_ref[...], g2_ref[...]).astype(BF)
        x2_scr[...] = x2
        qt = lax.dot_general(wqt_ref[...], x2, _NT, preferred_element_type=F32)
        for hp in range(2 * PEER_HEADS):
            st_scr[hp] = jnp.dot(keys_ref[hp % 2], qt[hp * PEER_NKEYS:(hp + 1) * PEER_NKEYS].astype(BF),
                                 preferred_element_type=F32)

        def half_body(hp, carry):
            for lt in range(tn // lanes):
                ls = slice(lt * lanes, (lt + 1) * lanes)
                vals = _top_values(st_scr[hp, :, ls], nk)
                for k in range(nk):
                    sv_scr[hp, k:k + 1, ls] = vals[k]
                sv_scr[hp, nk:24, ls] = jnp.full((24 - nk, lanes), -jnp.inf, F32)
            return carry

        lax.fori_loop(0, 2 * PEER_HEADS, half_body, 0)

        row8 = lax.broadcasted_iota(jnp.int32, (8, lanes), 0)

        def head_body(h, carry):
            for lt in range(tn // lanes):
                ls = slice(lt * lanes, (lt + 1) * lanes)
                sv1 = sv_scr[2 * h, :, ls]
                sv2 = sv_scr[2 * h + 1, :, ls]
                cands = [sv1[0:1] + sv2]
                for r in range(2, nk + 1):
                    cands.append(jnp.where(row8 < nk // r, sv1[r - 1:r] + sv2[0:8], -jnp.inf))
                cv = _top_values(jnp.concatenate(cands, axis=0), nk)
                z = jnp.ones_like(cv[0])
                for k in range(1, PEER_TOPK):
                    z = z + jnp.exp(cv[k] - cv[0])
                tau = 0.5 * (cv[PEER_TOPK - 1] + cv[PEER_TOPK])
                hz = 0.5 / z
                cz_scr[h, :, ls] = jnp.broadcast_to(jnp.exp(tau - cv[0]) * hz, (8, lanes))
                e1_scr[h, :, ls] = jnp.exp(st_scr[2 * h, :, ls] - sv1[0:1]) * hz
                e2_scr[h, :, ls] = jnp.exp(st_scr[2 * h + 1, :, ls] - sv2[0:1])
            return carry

        lax.fori_loop(0, PEER_HEADS, head_body, 0)
        acc_scr[...] = jnp.zeros_like(acc_scr)

    ch = PEER_CHUNK
    n_chunks = PEER_EXPERTS // ch

    def act_piece(c, nh):
        ns = slice(nh * PEER_LANES, (nh + 1) * PEER_LANES)
        at_scr[c % 2, :, ns] = lax.dot_general(u_ref[c * ch:(c + 1) * ch, :], x2_scr[ns, :], _NT,
                                               preferred_element_type=F32)

    def value_piece(c, mq):
        ms = slice(mq * PEER_VROWS, (mq + 1) * PEER_VROWS)
        acc_scr[ms, :] += jnp.dot(vt_ref[ms, c * ch:(c + 1) * ch], mt_scr[c % 2], preferred_element_type=F32)

    def gate_blocks(c):
        blocks = []
        for a in range(ch // PEER_NKEYS):
            for lw in range(tn // PEER_LANES):
                shared = {}

                def block(rb, a=a, lw=lw, shared=shared):
                    ls = slice(lw * PEER_LANES, (lw + 1) * PEER_LANES)
                    if not shared:
                        i1 = j * (PEER_EXPERTS // PEER_NKEYS) + (c * ch) // PEER_NKEYS + a
                        shared["e1"] = [jnp.broadcast_to(e1_scr[h, pl.ds(i1, 1), ls], (PEER_ROWS, PEER_LANES))
                                        for h in range(PEER_HEADS)]
                        shared["cz"] = [jnp.concatenate([cz_scr[h, :, ls]] * (PEER_ROWS // 8), axis=0)
                                        for h in range(PEER_HEADS)]
                    i2 = rb * PEER_ROWS
                    r0 = a * PEER_NKEYS + i2
                    xa = at_scr[c % 2, r0:r0 + PEER_ROWS, ls]
                    act = xa + xa * lax.erf(xa * (1.0 / math.sqrt(2.0)))
                    w = jnp.zeros((PEER_ROWS, PEER_LANES), F32)
                    for h in range(PEER_HEADS):
                        p = shared["e1"][h] * e2_scr[h, i2:i2 + PEER_ROWS, ls]
                        w = w + jnp.where(p >= shared["cz"][h], p, 0.0)
                    mt_scr[c % 2, r0:r0 + PEER_ROWS, ls] = (w * act).astype(BF)

                blocks += [functools.partial(block, rb) for rb in range(PEER_NKEYS // PEER_ROWS)]
        return blocks

    for nh in range(tn // PEER_LANES):
        act_piece(0, nh)
    for c in range(n_chunks):
        pieces = []
        if c + 1 < n_chunks:
            pieces += [functools.partial(act_piece, c + 1, nh) for nh in range(tn // PEER_LANES)]
        if c >= 1:
            pieces += [functools.partial(value_piece, c - 1, mq) for mq in range(D_MODEL // PEER_VROWS)]
        blocks = gate_blocks(c)
        for bi, blk in enumerate(blocks):
            for pi, piece in enumerate(pieces):
                if (pi * len(blocks)) // len(pieces) == bi:
                    piece()
            blk()
    for mq in range(D_MODEL // PEER_VROWS):
        value_piece(n_chunks - 1, mq)

    @pl.when(j == n_esteps - 1)
    def _():
        h2 = h_ref[...] + acc_scr[...].T
        y_ref[...] = _rms(h2, gf_ref[...])


def _peer_final(h2d, g2, wqt, keys, u_bf, vt_bf, gf):
    rows = h2d.shape[0]
    tn = PEER_TOKENS
    n_esteps = PEER_N_EXPERTS // PEER_EXPERTS
    hs = pl.BlockSpec((tn, D_MODEL), lambda i, j: (i, 0))
    return pl.pallas_call(
        functools.partial(_peer_kernel, n_esteps),
        grid=(rows // tn, n_esteps),
        in_specs=[hs, _const((1, D_MODEL)), _const((2 * PEER_HEADS * PEER_NKEYS, D_MODEL)),
                  _const((2, PEER_NKEYS, PEER_NKEYS)),
                  pl.BlockSpec((PEER_EXPERTS, D_MODEL), lambda i, j: (j, 0)),
                  pl.BlockSpec((D_MODEL, PEER_EXPERTS), lambda i, j: (0, j)), _const((1, D_MODEL))],
        out_specs=hs,
        out_shape=jax.ShapeDtypeStruct((rows, D_MODEL), F32),
        scratch_shapes=[pltpu.VMEM((tn, D_MODEL), BF),
                        pltpu.VMEM((2 * PEER_HEADS, PEER_NKEYS, tn), F32),
                        pltpu.VMEM((2 * PEER_HEADS, 24, tn), F32),
                        pltpu.VMEM((PEER_HEADS, PEER_NKEYS, tn), F32),
                        pltpu.VMEM((PEER_HEADS, PEER_NKEYS, tn), F32),
                        pltpu.VMEM((PEER_HEADS, 8, tn), F32),
                        pltpu.VMEM((2, PEER_CHUNK, tn), F32),
                        pltpu.VMEM((2, PEER_CHUNK, tn), BF),
                        pltpu.VMEM((D_MODEL, tn), F32)],
        compiler_params=_params(("parallel", "arbitrary")),
        name="peer_final",
    )(h2d, g2, wqt, keys, u_bf, vt_bf, gf)


def _block_diag(w):
    g, r, c = w.shape
    idx = jnp.arange(g)
    return jnp.zeros((g, r, g, c), w.dtype).at[idx, :, idx, :].set(w).reshape(g * r, g * c)


def _row(v):
    return v.reshape(1, -1).astype(F32)


def kernel(x_prompt, x_sample, mem_prompt, state_ssm_re, state_ssm_im, state_conv, cache_mem_k, cache_mem_v,
           norm1_g, w_in, b_in, s5_log_dt, s5_a_re, s5_a_im, s5_b_re, s5_b_im, s5_c_re, s5_c_im, s5_d,
           s5_glu_w, s5_glu_b, w_s5_proj, conv_dw_w, conv_dw_b, conv_ln_g, conv_ln_b, w_conv_proj, mem_norm_g,
           w_mem_k, w_mem_v, w_xa_proj, w_out, norm2_g, peer_w_q, peer_sub_keys, peer_u, peer_v, final_norm_g):
    assert norm1_g.shape[0] == 1, "single-layer trunk"
    bp, tp, _ = x_prompt.shape
    bs, ts, _ = x_sample.shape
    l = 0

    w1 = w_in[l][:, :4 * WIDTH].astype(BF)
    b1 = _row(b_in[l][:4 * WIDTH])
    wg = w_in[l][:, 4 * WIDTH:].astype(BF)
    bg = _row(b_in[l][4 * WIDTH:])
    sw = {
        "bmat": jnp.concatenate([_block_diag(jnp.swapaxes(s5_b_re[l], 1, 2)),
                                 _block_diag(jnp.swapaxes(s5_b_im[l], 1, 2))], axis=1).astype(BF),
        "cmat": jnp.concatenate([_block_diag(jnp.swapaxes(s5_c_re[l], 1, 2)),
                                 -_block_diag(jnp.swapaxes(s5_c_im[l], 1, 2))], axis=0).astype(BF),
        "dskip": _row(s5_d[l]),
        "ldt": _row(jnp.repeat(s5_log_dt[l], S5_STATE)),
        "are": _row(s5_a_re[l]),
        "aim": _row(s5_a_im[l]),
        "dww": conv_dw_w[l].astype(F32),
        "dwb": _row(conv_dw_b[l]),
        "lng": _row(conv_ln_g[l]),
        "lnb": _row(conv_ln_b[l]),
        "gluw": s5_glu_w[l].astype(BF),
        "glub": _row(s5_glu_b[l]),
    }
    g1 = _row(norm1_g[l])
    merge_w = (g1, wg, bg, w_s5_proj[l].astype(BF), w_conv_proj[l].astype(BF), w_xa_proj[l].astype(BF),
               w_out[l].astype(BF))
    peer_w = (_row(norm2_g[l]), peer_w_q[l].T.astype(BF), peer_sub_keys[l].astype(BF),
              peer_u.reshape(PEER_N_EXPERTS, D_MODEL).astype(BF),
              peer_v.reshape(PEER_N_EXPERTS, D_MODEL).T.astype(BF), _row(final_norm_g))

    mk, mv = _memkv(mem_prompt.reshape(bp * N_MEM, D_MODEL), _row(mem_norm_g[l]),
                    w_mem_k[l].astype(BF), w_mem_v[l].astype(BF))
    u2, a2, q3 = _inproj(x_prompt, g1, w1, b1)
    zeros_st = jnp.zeros((bp, S5_CH), F32)
    s5_tm, cv_tm, re_p, im_p = _seq(u2.reshape(tp * bp, WIDTH), a2.reshape(tp * bp, WIDTH), bp, 64, 512,
                                    zeros_st, zeros_st, jnp.zeros((HIST * bp, WIDTH), F32), sw)
    xa3 = _attn_prompt(q3, mk.reshape(bp, N_MEM, WIDTH), mv.reshape(bp, N_MEM, WIDTH))
    h_p = _merge(x_prompt, s5_tm.reshape(tp, bp * WIDTH), cv_tm.reshape(tp, bp * WIDTH), xa3, *merge_w)
    y_prompt = _peer_final(h_p.reshape(bp * tp, D_MODEL), *peer_w).reshape(bp, tp, D_MODEL)
    conv_p = jnp.swapaxes(a2[tp - HIST:].reshape(HIST, bp, WIDTH), 0, 1)

    n_s = bs * ts
    us, as_, qs = _inproj(x_sample.reshape(1, n_s, D_MODEL), g1, w1, b1)

    def to_tm(v):
        return jnp.swapaxes(v.reshape(bs, ts, WIDTH), 0, 1).reshape(n_s, WIDTH)

    def to_bm(v):
        return jnp.swapaxes(v.reshape(ts, bs, WIDTH), 0, 1).reshape(n_s, WIDTH)

    hist_s = jnp.swapaxes(state_conv[l], 0, 1).reshape(HIST * bs, WIDTH)
    s5_s, cv_s, re_s, im_s = _seq(to_tm(us), to_tm(as_), bs, ts, 128,
                                  state_ssm_re[l].reshape(bs, S5_CH), state_ssm_im[l].reshape(bs, S5_CH), hist_s, sw)
    xa_s = _attn_sample(qs.reshape(bs, ts, WIDTH).astype(F32), cache_mem_k[l].reshape(bs, N_MEM, WIDTH),
                        cache_mem_v[l].reshape(bs, N_MEM, WIDTH))
    h_s = _merge(x_sample.reshape(1, n_s, D_MODEL), to_bm(s5_s), to_bm(cv_s),
                 xa_s.reshape(1, n_s, WIDTH).astype(BF), *merge_w)
    y_sample = _peer_final(h_s.reshape(n_s, D_MODEL), *peer_w).reshape(bs, ts, D_MODEL)
    conv_s = jnp.concatenate([state_conv[l][:, ts:], as_.reshape(bs, ts, WIDTH)], axis=1)

    st_shape_p = (1, bp, S5_GROUPS, S5_STATE)
    st_shape_s = (1, bs, S5_GROUPS, S5_STATE)
    return (y_prompt, y_sample, re_p.reshape(st_shape_p), im_p.reshape(st_shape_p), conv_p[None],
            mk.reshape(1, bp, N_MEM, XA_HEADS, XA_HEAD_DIM), mv.reshape(1, bp, N_MEM, XA_HEADS, XA_HEAD_DIM),
            re_s.reshape(st_shape_s), im_s.reshape(st_shape_s), conv_s[None])
```

```python
import functools
import math

import jax
import jax.numpy as jnp
from jax import lax
from jax.experimental import pallas as pl
from jax.experimental.pallas import tpu as pltpu

F32 = jnp.float32
BF = jnp.bfloat16

D_MODEL = 1024
WIDTH = 512
S5_GROUPS = 32
S5_GROUP = 16
S5_STATE = 64
S5_CH = S5_GROUPS * S5_STATE
S5_SUPER = 2
CONV_K = 31
HIST = CONV_K - 1
N_MEM = 256
XA_HEADS = 4
XA_HEAD_DIM = 128
PEER_HEADS = 8
PEER_NKEYS = 128
PEER_N_EXPERTS = PEER_NKEYS * PEER_NKEYS
PEER_TOPK = 16
RMS_EPS = 1e-6
LN_EPS = 1e-5

ROW_TILE = 512
PEER_TOKENS = 512
PEER_EXPERTS = 2048
PEER_CHUNK = 256
PEER_ROWS = 16
PEER_LANES = 256
PEER_VROWS = 256
VMEM_LIMIT = 56 * 1024 * 1024

_NT = (((1,), (1,)), ((), ()))


def _params(sem):
    return pltpu.CompilerParams(dimension_semantics=sem, vmem_limit_bytes=VMEM_LIMIT)


def _rms(x, g):
    return x * lax.rsqrt(jnp.mean(x * x, axis=-1, keepdims=True) + RMS_EPS) * g


def _sigmoid(x):
    return 1.0 / (1.0 + jnp.exp(-x))


def _gelu(x):
    return 0.5 * x * (1.0 + lax.erf(x * (1.0 / math.sqrt(2.0))))


def _const(shape):
    nd = len(shape)
    return pl.BlockSpec(shape, lambda *_: (0,) * nd, pipeline_mode=pl.Buffered(1))


def _memkv_kernel(m_ref, g_ref, wk_ref, wv_ref, k_ref, v_ref):
    m = _rms(m_ref[...], g_ref[...]).astype(BF)
    k_ref[...] = jnp.dot(m, wk_ref[...], preferred_element_type=F32)
    v_ref[...] = jnp.dot(m, wv_ref[...], preferred_element_type=F32)


def _memkv(mem2d, g, wk, wv):
    rows = mem2d.shape[0]
    return pl.pallas_call(
        _memkv_kernel,
        grid=(rows // ROW_TILE,),
        in_specs=[pl.BlockSpec((ROW_TILE, D_MODEL), lambda i: (i, 0)), _const((1, D_MODEL)),
                  _const((D_MODEL, WIDTH)), _const((D_MODEL, WIDTH))],
        out_specs=[pl.BlockSpec((ROW_TILE, WIDTH), lambda i: (i, 0))] * 2,
        out_shape=[jax.ShapeDtypeStruct((rows, WIDTH), F32)] * 2,
        compiler_params=_params(("parallel",)),
        name="mem_kv",
    )(mem2d, g, wk, wv)


def _seq_kernel(nb, tt, cw, n_steps,
                x_ref, g1_ref, w1_ref, b1_ref, p_ref, pt_ref, bmat_ref, cmat_ref, dsk_ref, ldt_ref, are_ref, aim_ref,
                h0r_ref, h0i_ref, hist_ref, dww_ref, dwb_ref, lng_ref, lnb_ref, gluw_ref, glub_ref,
                q_ref, s5_ref, co_ref, a_ref, sre_ref, sim_ref,
                bu_scr, st_scr, par_scr, cb_scr, y_scr):
    i = pl.program_id(0)
    rt = nb * tt
    half = S5_CH // S5_SUPER
    uw = WIDTH // S5_SUPER

    @pl.when(i == 0)
    def _():
        st_scr[0] = h0r_ref[...]
        st_scr[1] = h0i_ref[...]
        cb_scr[0:HIST * nb, :] = hist_ref[...]

    dt = jnp.exp(ldt_ref[...])
    are = are_ref[...]
    aim = aim_ref[...]
    mag = jnp.exp(are * dt)
    lr = mag * jnp.cos(aim * dt)
    li = mag * jnp.sin(aim * dt)
    den = are * are + aim * aim
    nr = lr - 1.0
    par_scr[0:1, :] = lr
    par_scr[1:2, :] = li
    par_scr[2:3, :] = (nr * are + li * aim) / den
    par_scr[3:4, :] = (li * are - nr * aim) / den

    n_bm = _rms(x_ref[...].reshape(rt, D_MODEL), g1_ref[...]).astype(BF)
    n_tm = jnp.dot(p_ref[...], n_bm, preferred_element_type=F32).astype(BF)
    z = jnp.dot(n_tm, w1_ref[...], preferred_element_type=F32) + b1_ref[...]
    u = z[:, :WIDTH]
    a = z[:, WIDTH:2 * WIDTH] * _sigmoid(z[:, 2 * WIDTH:3 * WIDTH])
    q_tm = z[:, 3 * WIDTH:4 * WIDTH].astype(BF)
    q_ref[...] = jnp.dot(pt_ref[...], q_tm, preferred_element_type=F32).astype(BF).reshape(q_ref.shape)
    a_ref[...] = a

    u_bf = u.astype(BF)
    for s in range(S5_SUPER):
        bu_scr[:, 2 * half * s:2 * half * (s + 1)] = jnp.dot(u_bf[:, uw * s:uw * (s + 1)], bmat_ref[s],
                                                             preferred_element_type=F32)

    for s in range(S5_SUPER):
        for c in range(half // cw):
            nat = slice(half * s + c * cw, half * s + (c + 1) * cw)
            re_sl = slice(2 * half * s + c * cw, 2 * half * s + (c + 1) * cw)
            im_sl = slice(2 * half * s + half + c * cw, 2 * half * s + half + (c + 1) * cw)
            lr_b = jnp.broadcast_to(par_scr[0:1, nat], (nb, cw))
            li_b = jnp.broadcast_to(par_scr[1:2, nat], (nb, cw))
            cr_b = jnp.broadcast_to(par_scr[2:3, nat], (nb, cw))
            ci_b = jnp.broadcast_to(par_scr[3:4, nat], (nb, cw))

            def step(t, carry, re_sl=re_sl, im_sl=im_sl, lr_b=lr_b, li_b=li_b, cr_b=cr_b, ci_b=ci_b):
                s_re, s_im = carry
                r0 = pl.multiple_of(t * nb, nb)
                rr = bu_scr[pl.ds(r0, nb), re_sl]
                ri = bu_scr[pl.ds(r0, nb), im_sl]
                n_re = lr_b * s_re - li_b * s_im + (cr_b * rr - ci_b * ri)
                n_im = lr_b * s_im + li_b * s_re + (cr_b * ri + ci_b * rr)
                bu_scr[pl.ds(r0, nb), re_sl] = n_re
                bu_scr[pl.ds(r0, nb), im_sl] = n_im
                return n_re, n_im

            s_re, s_im = lax.fori_loop(0, tt, step, (st_scr[0, :, nat], st_scr[1, :, nat]), unroll=min(tt, 8))
            st_scr[0, :, nat] = s_re
            st_scr[1, :, nat] = s_im

    y = jnp.concatenate(
        [jnp.dot(bu_scr[:, 2 * half * s:2 * half * (s + 1)].astype(BF), cmat_ref[s], preferred_element_type=F32)
         for s in range(S5_SUPER)], axis=-1) + dsk_ref[...] * u
    zz = _gelu(y)
    gate = _sigmoid(jnp.dot(zz.astype(BF), gluw_ref[...], preferred_element_type=F32) + glub_ref[...])
    s5_tm = (zz * gate).astype(BF)
    s5_ref[...] = jnp.dot(pt_ref[...], s5_tm, preferred_element_type=F32).astype(BF).reshape(s5_ref.shape)

    cb_scr[HIST * nb:HIST * nb + rt, :] = a
    rb = 32

    def cblock(bi, carry):
        r0 = pl.multiple_of(bi * rb, rb)
        acc = jnp.zeros((rb, WIDTH), F32)
        for k in range(CONV_K):
            acc = acc + dww_ref[k:k + 1, :] * cb_scr[pl.ds(r0 + k * nb, rb), :]
        y_scr[pl.ds(r0, rb), :] = acc
        return carry

    lax.fori_loop(0, rt // rb, cblock, 0)
    if n_steps > 1:
        cb_scr[0:HIST * nb, :] = cb_scr[rt:rt + HIST * nb, :]

    yv = y_scr[...] + dwb_ref[...]
    yc = yv - jnp.mean(yv, axis=-1, keepdims=True)
    var = jnp.mean(yc * yc, axis=-1, keepdims=True)
    ln = yc * lax.rsqrt(var + LN_EPS) * lng_ref[...] + lnb_ref[...]
    co_tm = (ln * _sigmoid(ln)).astype(BF)
    co_ref[...] = jnp.dot(pt_ref[...], co_tm, preferred_element_type=F32).astype(BF).reshape(co_ref.shape)

    @pl.when(i == n_steps - 1)
    def _():
        sre_ref[...] = st_scr[0]
        sim_ref[...] = st_scr[1]


def _seq(x, nb, tt, cw, h0r, h0i, hist, g1, w1, b1, sw):
    rt = nb * tt
    if x.ndim == 3:
        n_steps = x.shape[1] // tt
        xspec = pl.BlockSpec((nb, tt, D_MODEL), lambda i: (0, i, 0))
        ospec = pl.BlockSpec((nb, tt, WIDTH), lambda i: (0, i, 0))
        oshape = jax.ShapeDtypeStruct((nb, x.shape[1], WIDTH), BF)
    else:
        n_steps = 1
        xspec = pl.BlockSpec((rt, D_MODEL), lambda i: (0, 0))
        ospec = pl.BlockSpec((rt, WIDTH), lambda i: (0, 0))
        oshape = jax.ShapeDtypeStruct((rt, WIDTH), BF)
    col = jnp.arange(rt)
    perm = (jnp.arange(rt)[:, None] == ((col % tt) * nb + col // tt)[None, :]).astype(BF)
    st = _const((nb, S5_CH))
    vec = _const((1, WIDTH))
    chv = _const((1, S5_CH))
    half = S5_CH // S5_SUPER
    return pl.pallas_call(
        functools.partial(_seq_kernel, nb, tt, cw, n_steps),
        grid=(n_steps,),
        in_specs=[xspec, _const((1, D_MODEL)), _const((D_MODEL, 4 * WIDTH)), _const((1, 4 * WIDTH)),
                  _const((rt, rt)), _const((rt, rt)),
                  _const((S5_SUPER, WIDTH // S5_SUPER, 2 * half)), _const((S5_SUPER, 2 * half, WIDTH // S5_SUPER)),
                  vec, chv, chv, chv, st, st, _const((HIST * nb, WIDTH)), _const((CONV_K, WIDTH)), vec, vec, vec,
                  _const((WIDTH, WIDTH)), vec],
        out_specs=[ospec, ospec, ospec, _const((rt, WIDTH)), st, st],
        out_shape=[oshape, oshape, oshape, jax.ShapeDtypeStruct((rt, WIDTH), F32),
                   jax.ShapeDtypeStruct((nb, S5_CH), F32), jax.ShapeDtypeStruct((nb, S5_CH), F32)],
        scratch_shapes=[pltpu.VMEM((rt, 2 * S5_CH), F32), pltpu.VMEM((2, nb, S5_CH), F32),
                        pltpu.VMEM((8, S5_CH), F32), pltpu.VMEM(((HIST + tt) * nb, WIDTH), F32),
                        pltpu.VMEM((rt, WIDTH), F32)],
        compiler_params=_params(("arbitrary",)),
        name="s5_conv",
    )(x, g1, w1, b1, perm, perm.T, sw["bmat"], sw["cmat"], sw["dskip"], sw["ldt"], sw["are"], sw["aim"],
      h0r, h0i, hist, sw["dww"], sw["dwb"], sw["lng"], sw["lnb"], sw["gluw"], sw["glub"])


def _attn_heads(q, k, v):
    scale = XA_HEAD_DIM ** -0.5
    outs = []
    for h in range(XA_HEADS):
        sl = slice(h * XA_HEAD_DIM, (h + 1) * XA_HEAD_DIM)
        s = lax.dot_general(q[:, sl], k[:, sl], _NT, preferred_element_type=F32) * scale
        e = jnp.exp(s - jnp.max(s, axis=-1, keepdims=True))
        p = e / jnp.sum(e, axis=-1, keepdims=True)
        outs.append(jnp.dot(p.astype(BF), v[:, sl], preferred_element_type=F32))
    return jnp.concatenate(outs, axis=-1)


def _attn_prompt_kernel(q_ref, k_ref, v_ref, o_ref):
    o_ref[...] = _attn_heads(q_ref[...], k_ref[...].astype(BF), v_ref[...].astype(BF)).astype(o_ref.dtype)


def _attn_prompt(q3, k3, v3):
    bn, tn, _ = q3.shape
    tm = min(ROW_TILE, tn)
    kv = pl.BlockSpec((None, N_MEM, WIDTH), lambda bi, i: (bi, 0, 0))
    qs = pl.BlockSpec((None, tm, WIDTH), lambda bi, i: (bi, i, 0))
    return pl.pallas_call(
        _attn_prompt_kernel,
        grid=(bn, tn // tm),
        in_specs=[qs, kv, kv],
        out_specs=qs,
        out_shape=jax.ShapeDtypeStruct((bn, tn, WIDTH), BF),
        compiler_params=_params(("parallel", "parallel")),
        name="xattn_prompt",
    )(q3, k3, v3)


def _attn_sample_kernel(bb, tq, q_ref, k_ref, v_ref, o_ref):
    nr = XA_HEADS * tq
    row_head = lax.broadcasted_iota(jnp.int32, (nr, WIDTH), 0) // tq
    lane_head = lax.broadcasted_iota(jnp.int32, (nr, WIDTH), 1) // XA_HEAD_DIM
    own = row_head == lane_head
    scale = XA_HEAD_DIM ** -0.5
    for b in range(bb):
        qb = q_ref[b]
        qrows = jnp.where(own, jnp.concatenate([qb] * XA_HEADS, axis=0), 0.0).astype(BF)
        s = lax.dot_general(qrows, k_ref[b].astype(BF), _NT, preferred_element_type=F32) * scale
        e = jnp.exp(s - jnp.max(s, axis=-1, keepdims=True))
        p = e / jnp.sum(e, axis=-1, keepdims=True)
        of = jnp.where(own, jnp.dot(p.astype(BF), v_ref[b].astype(BF), preferred_element_type=F32), 0.0)
        ob = of[0:tq]
        for h in range(1, XA_HEADS):
            ob = ob + of[h * tq:(h + 1) * tq]
        o_ref[b] = ob


def _attn_sample(q3, k3, v3):
    bn, tq, _ = q3.shape
    bb = 8
    kv = pl.BlockSpec((bb, N_MEM, WIDTH), lambda i: (i, 0, 0))
    qs = pl.BlockSpec((bb, tq, WIDTH), lambda i: (i, 0, 0))
    return pl.pallas_call(
        functools.partial(_attn_sample_kernel, bb, tq),
        grid=(bn // bb,),
        in_specs=[qs, kv, kv],
        out_specs=qs,
        out_shape=jax.ShapeDtypeStruct((bn, tq, WIDTH), F32),
        compiler_params=_params(("parallel",)),
        name="xattn_sample",
    )(q3, k3, v3)


def _merge_kernel(x_ref, s5_ref, cv_ref, xa_ref, g_ref, wg_ref, bg_ref, ws5_ref, wcv_ref, wxa_ref, wout_ref, h_ref):
    x = x_ref[...]
    n = _rms(x, g_ref[...]).astype(BF)
    gate = _sigmoid(jnp.dot(n, wg_ref[...], preferred_element_type=F32) + bg_ref[...])
    merged = (gate[:, :D_MODEL] * jnp.dot(s5_ref[...], ws5_ref[...], preferred_element_type=F32)
              + gate[:, D_MODEL:2 * D_MODEL] * jnp.dot(cv_ref[...], wcv_ref[...], preferred_element_type=F32)
              + gate[:, 2 * D_MODEL:] * jnp.dot(xa_ref[...], wxa_ref[...], preferred_element_type=F32))
    h_ref[...] = x + jnp.dot(merged.astype(BF), wout_ref[...], preferred_element_type=F32)


def _merge(x2, s5, cv, xa, g, wg, bg, ws5, wcv, wxa, wout):
    rows = x2.shape[0]
    xs = pl.BlockSpec((ROW_TILE, D_MODEL), lambda i: (i, 0))
    br = pl.BlockSpec((ROW_TILE, WIDTH), lambda i: (i, 0))
    proj = _const((WIDTH, D_MODEL))
    return pl.pallas_call(
        _merge_kernel,
        grid=(rows // ROW_TILE,),
        in_specs=[xs, br, br, br, _const((1, D_MODEL)), _const((D_MODEL, 3 * D_MODEL)), _const((1, 3 * D_MODEL)),
                  proj, proj, proj, _const((D_MODEL, D_MODEL))],
        out_specs=xs,
        out_shape=jax.ShapeDtypeStruct((rows, D_MODEL), F32),
        compiler_params=_params(("parallel",)),
        name="merge_out",
    )(x2, s5, cv, xa, g, wg, bg, ws5, wcv, wxa, wout)


def _top_values(work, count):
    vals = []
    for _ in range(count):
        m = jnp.max(work, axis=0, keepdims=True)
        vals.append(m)
        work = jnp.where(work == m, -jnp.inf, work)
    return vals


def _sorting_network(n):
    pairs = []

    def merge(lo, cnt, r):
        step = 2 * r
        if step < cnt:
            merge(lo, cnt, step)
            merge(lo + r, cnt, step)
            pairs.extend((k, k + r) for k in range(lo + r, lo + cnt - r, step))
        else:
            pairs.append((lo, lo + r))

    def sort(lo, cnt):
        if cnt > 1:
            sort(lo, cnt // 2)
            sort(lo + cnt // 2, cnt // 2)
            merge(lo, cnt, 1)

    sort(0, n)
    return pairs


def _top_values_sorted(work, count):
    cols = [work[8 * k:8 * (k + 1)] for k in range(work.shape[0] // 8)]
    for lo, hi in _sorting_network(len(cols)):
        cols[lo], cols[hi] = jnp.maximum(cols[lo], cols[hi]), jnp.minimum(cols[lo], cols[hi])
    vals = []
    for k in range(count):
        m = jnp.max(cols[0], axis=0, keepdims=True)
        vals.append(m)
        needed = count - 1 - k
        if needed == 0:
            break
        hit = cols[0] == m
        cols = [jnp.where(hit, cols[d + 1] if d + 1 < len(cols) else -jnp.inf, cols[d])
                for d in range(min(len(cols), needed))]
    return vals


def _peer_kernel(n_esteps, h_ref, g2_ref, wqt_ref, keys_ref, u_ref, vt_ref, gf_ref, y_ref,
                 x2_scr, st_scr, sv_scr, e1_scr, e2_scr, cz_scr, at_scr, mt_scr, acc_scr):
    j = pl.program_id(1)
    tn = PEER_TOKENS
    nk = PEER_TOPK + 1
    lanes = 128

    @pl.when(j == 0)
    def _():
        x2 = _rms(h_ref[...], g2_ref[...]).astype(BF)
        x2_scr[...] = x2
        qt = lax.dot_general(wqt_ref[...], x2, _NT, preferred_element_type=F32)
        for hp in range(2 * PEER_HEADS):
            st_scr[hp] = jnp.dot(keys_ref[hp % 2], qt[hp * PEER_NKEYS:(hp + 1) * PEER_NKEYS].astype(BF),
                                 preferred_element_type=F32)

        def half_body(hp, carry):
            for lt in range(tn // lanes):
                ls = slice(lt * lanes, (lt + 1) * lanes)
                vals = _top_values_sorted(st_scr[hp, :, ls], nk)
                for k in range(nk):
                    sv_scr[hp, k:k + 1, ls] = vals[k]
                sv_scr[hp, nk:24, ls] = jnp.full((24 - nk, lanes), -jnp.inf, F32)
            return carry

        lax.fori_loop(0, 2 * PEER_HEADS, half_body, 0)

        row8 = lax.broadcasted_iota(jnp.int32, (8, lanes), 0)

        def head_body(h, carry):
            for lt in range(tn // lanes):
                ls = slice(lt * lanes, (lt + 1) * lanes)
                sv1 = sv_scr[2 * h, :, ls]
                sv2 = sv_scr[2 * h + 1, :, ls]
                cands = [sv1[0:1] + sv2, sv1 + sv2[0:1]]
                r_max = math.isqrt(nk)
                for r in range(2, r_max + 1):
                    cands.append(jnp.where(row8 < nk // r, sv1[r - 1:r] + sv2[0:8], -jnp.inf))
                for jj in range(2, nk // (r_max + 1) + 1):
                    cands.append(jnp.where(row8 < nk // jj, sv1[0:8] + sv2[jj - 1:jj], -jnp.inf))
                cv = _top_values(jnp.concatenate(cands, axis=0), nk)
                z = jnp.ones_like(cv[0])
                for k in range(1, PEER_TOPK):
                    z = z + jnp.exp(cv[k] - cv[0])
                tau = 0.5 * (cv[PEER_TOPK - 1] + cv[PEER_TOPK])
                hz = 0.5 / z
                cz_scr[h, :, ls] = jnp.broadcast_to(jnp.exp(tau - cv[0]) * hz, (8, lanes))
                e1_scr[h, :, ls] = jnp.exp(st_scr[2 * h, :, ls] - sv1[0:1]) * hz
                e2_scr[h, :, ls] = jnp.exp(st_scr[2 * h + 1, :, ls] - sv2[0:1])
            return carry

        lax.fori_loop(0, PEER_HEADS, head_body, 0)
        acc_scr[...] = jnp.zeros_like(acc_scr)

    ch = PEER_CHUNK
    n_chunks = PEER_EXPERTS // ch

    def act_piece(c, nh):
        ns = slice(nh * PEER_LANES, (nh + 1) * PEER_LANES)
        at_scr[c % 2, :, ns] = lax.dot_general(u_ref[c * ch:(c + 1) * ch, :], x2_scr[ns, :], _NT,
                                               preferred_element_type=F32)

    def value_piece(c, mq):
        ms = slice(mq * PEER_VROWS, (mq + 1) * PEER_VROWS)
        acc_scr[ms, :] += jnp.dot(vt_ref[ms, c * ch:(c + 1) * ch], mt_scr[c % 2], preferred_element_type=F32)

    def gate_blocks(c):
        blocks = []
        for a in range(ch // PEER_NKEYS):
            for lw in range(tn // PEER_LANES):
                shared = {}

                def block(rb, a=a, lw=lw, shared=shared):
                    ls = slice(lw * PEER_LANES, (lw + 1) * PEER_LANES)
                    if not shared:
                        i1 = j * (PEER_EXPERTS // PEER_NKEYS) + (c * ch) // PEER_NKEYS + a
                        shared["e1"] = [jnp.broadcast_to(e1_scr[h, pl.ds(i1, 1), ls], (PEER_ROWS, PEER_LANES))
                                        for h in range(PEER_HEADS)]
                        shared["cz"] = [jnp.concatenate([cz_scr[h, :, ls]] * (PEER_ROWS // 8), axis=0)
                                        for h in range(PEER_HEADS)]
                    i2 = rb * PEER_ROWS
                    r0 = a * PEER_NKEYS + i2
                    xa = at_scr[c % 2, r0:r0 + PEER_ROWS, ls]
                    act = xa + xa * lax.erf(xa * (1.0 / math.sqrt(2.0)))
                    w = jnp.zeros((PEER_ROWS, PEER_LANES), F32)
                    for h in range(PEER_HEADS):
                        p = shared["e1"][h] * e2_scr[h, i2:i2 + PEER_ROWS, ls]
                        w = w + jnp.where(p >= shared["cz"][h], p, 0.0)
                    mt_scr[c % 2, r0:r0 + PEER_ROWS, ls] = (w * act).astype(BF)

                blocks += [functools.partial(block, rb) for rb in range(PEER_NKEYS // PEER_ROWS)]
        return blocks

    for nh in range(tn // PEER_LANES):
        act_piece(0, nh)
    for c in range(n_chunks):
        pieces = []
        if c + 1 < n_chunks:
            pieces += [functools.partial(act_piece, c + 1, nh) for nh in range(tn // PEER_LANES)]
        if c >= 1:
            pieces += [functools.partial(value_piece, c - 1, mq) for mq in range(D_MODEL // PEER_VROWS)]
        blocks = gate_blocks(c)
        for bi, blk in enumerate(blocks):
            for pi, piece in enumerate(pieces):
                if (pi * len(blocks)) // len(pieces) == bi:
                    piece()
            blk()
    for mq in range(D_MODEL // PEER_VROWS):
        value_piece(n_chunks - 1, mq)

    @pl.when(j == n_esteps - 1)
    def _():
        h2 = h_ref[...] + acc_scr[...].T
        y_ref[...] = _rms(h2, gf_ref[...])


def _peer_final(h2d, g2, wqt, keys, u_bf, vt_bf, gf):
    rows = h2d.shape[0]
    tn = PEER_TOKENS
    n_esteps = PEER_N_EXPERTS // PEER_EXPERTS
    hs = pl.BlockSpec((tn, D_MODEL), lambda i, j: (i, 0))
    return pl.pallas_call(
        functools.partial(_peer_kernel, n_esteps),
        grid=(rows // tn, n_esteps),
        in_specs=[hs, _const((1, D_MODEL)), _const((2 * PEER_HEADS * PEER_NKEYS, D_MODEL)),
                  _const((2, PEER_NKEYS, PEER_NKEYS)),
                  pl.BlockSpec((PEER_EXPERTS, D_MODEL), lambda i, j: (j, 0)),
                  pl.BlockSpec((D_MODEL, PEER_EXPERTS), lambda i, j: (0, j)), _const((1, D_MODEL))],
        out_specs=hs,
        out_shape=jax.ShapeDtypeStruct((rows, D_MODEL), F32),
        scratch_shapes=[pltpu.VMEM((tn, D_MODEL), BF),
                        pltpu.VMEM((2 * PEER_HEADS, PEER_NKEYS, tn), F32),
                        pltpu.VMEM((2 * PEER_HEADS, 24, tn), F32),
                        pltpu.VMEM((PEER_HEADS, PEER_NKEYS, tn), F32),
                        pltpu.VMEM((PEER_HEADS, PEER_NKEYS, tn), F32),
                        pltpu.VMEM((PEER_HEADS, 8, tn), F32),
                        pltpu.VMEM((2, PEER_CHUNK, tn), F32),
                        pltpu.VMEM((2, PEER_CHUNK, tn), BF),
                        pltpu.VMEM((D_MODEL, tn), F32)],
        compiler_params=_params(("parallel", "arbitrary")),
        name="peer_final",
    )(h2d, g2, wqt, keys, u_bf, vt_bf, gf)


def _block_diag(w):
    g, r, c = w.shape
    idx = jnp.arange(g)
    return jnp.zeros((g, r, g, c), w.dtype).at[idx, :, idx, :].set(w).reshape(g * r, g * c)


def _row(v):
    return v.reshape(1, -1).astype(F32)


def kernel(x_prompt, x_sample, mem_prompt, state_ssm_re, state_ssm_im, state_conv, cache_mem_k, cache_mem_v,
           norm1_g, w_in, b_in, s5_log_dt, s5_a_re, s5_a_im, s5_b_re, s5_b_im, s5_c_re, s5_c_im, s5_d,
           s5_glu_w, s5_glu_b, w_s5_proj, conv_dw_w, conv_dw_b, conv_ln_g, conv_ln_b, w_conv_proj, mem_norm_g,
           w_mem_k, w_mem_v, w_xa_proj, w_out, norm2_g, peer_w_q, peer_sub_keys, peer_u, peer_v, final_norm_g):
    assert norm1_g.shape[0] == 1, "single-layer trunk"
    bp, tp, _ = x_prompt.shape
    bs, ts, _ = x_sample.shape
    n_s = bs * ts
    gps = S5_GROUPS // S5_SUPER

    w_in2 = w_in.reshape(D_MODEL, -1)
    b_in1 = b_in.reshape(-1)
    w1 = w_in2[:, :4 * WIDTH].astype(BF)
    b1 = _row(b_in1[:4 * WIDTH])
    wg = w_in2[:, 4 * WIDTH:].astype(BF)
    bg = _row(b_in1[4 * WIDTH:])

    def per_super(w):
        return jnp.stack([_block_diag(w[s * gps:(s + 1) * gps]) for s in range(S5_SUPER)])

    b_re_t = jnp.swapaxes(s5_b_re.reshape(S5_GROUPS, S5_STATE, S5_GROUP), 1, 2)
    b_im_t = jnp.swapaxes(s5_b_im.reshape(S5_GROUPS, S5_STATE, S5_GROUP), 1, 2)
    c_re_t = jnp.swapaxes(s5_c_re.reshape(S5_GROUPS, S5_GROUP, S5_STATE), 1, 2)
    c_im_t = jnp.swapaxes(s5_c_im.reshape(S5_GROUPS, S5_GROUP, S5_STATE), 1, 2)
    sw = {
        "bmat": jnp.concatenate([per_super(b_re_t), per_super(b_im_t)], axis=2).astype(BF),
        "cmat": jnp.concatenate([per_super(c_re_t), -per_super(c_im_t)], axis=1).astype(BF),
        "dskip": _row(s5_d),
        "ldt": _row(jnp.repeat(s5_log_dt.reshape(-1), S5_STATE)),
        "are": _row(s5_a_re),
        "aim": _row(s5_a_im),
        "dww": conv_dw_w.reshape(CONV_K, WIDTH).astype(F32),
        "dwb": _row(conv_dw_b),
        "lng": _row(conv_ln_g),
        "lnb": _row(conv_ln_b),
        "gluw": s5_glu_w.reshape(WIDTH, WIDTH).astype(BF),
        "glub": _row(s5_glu_b),
    }
    g1 = _row(norm1_g)
    merge_w = (g1, wg, bg, w_s5_proj.reshape(WIDTH, D_MODEL).astype(BF), w_conv_proj.reshape(WIDTH, D_MODEL).astype(BF),
               w_xa_proj.reshape(WIDTH, D_MODEL).astype(BF), w_out.reshape(D_MODEL, D_MODEL).astype(BF))
    peer_w = (_row(norm2_g), peer_w_q.reshape(D_MODEL, -1).T.astype(BF),
              peer_sub_keys.reshape(2, PEER_NKEYS, PEER_NKEYS).astype(BF),
              peer_u.reshape(PEER_N_EXPERTS, D_MODEL).astype(BF),
              peer_v.reshape(PEER_N_EXPERTS, D_MODEL).T.astype(BF), _row(final_norm_g))

    mk, mv = _memkv(mem_prompt.reshape(bp * N_MEM, D_MODEL), _row(mem_norm_g),
                    w_mem_k.reshape(D_MODEL, WIDTH).astype(BF), w_mem_v.reshape(D_MODEL, WIDTH).astype(BF))
    tt_p = 64
    zeros_st = jnp.zeros((bp, S5_CH), F32)
    q_p, s5_p, cv_p, a_last, re_p, im_p = _seq(x_prompt, bp, tt_p, 512, zeros_st, zeros_st,
                                               jnp.zeros((HIST * bp, WIDTH), F32), g1, w1, b1, sw)
    xa_p = _attn_prompt(q_p, mk.reshape(bp, N_MEM, WIDTH), mv.reshape(bp, N_MEM, WIDTH))
    h_p = _merge(x_prompt.reshape(bp * tp, D_MODEL), s5_p.reshape(bp * tp, WIDTH), cv_p.reshape(bp * tp, WIDTH),
                 xa_p.reshape(bp * tp, WIDTH), *merge_w)
    y_prompt = _peer_final(h_p, *peer_w).reshape(bp, tp, D_MODEL)
    conv_p = jnp.swapaxes(a_last[(tt_p - HIST) * bp:].reshape(HIST, bp, WIDTH), 0, 1)

    xs2 = x_sample.reshape(n_s, D_MODEL)
    st_conv = state_conv.reshape(bs, HIST, WIDTH)
    hist_s = jnp.swapaxes(st_conv, 0, 1).reshape(HIST * bs, WIDTH)
    q_s, s5_s, cv_s, a_s, re_s, im_s = _seq(xs2, bs, ts, 128, state_ssm_re.reshape(bs, S5_CH),
                                            state_ssm_im.reshape(bs, S5_CH), hist_s, g1, w1, b1, sw)
    xa_s = _attn_sample(q_s.reshape(bs, ts, WIDTH).astype(F32), cache_mem_k.reshape(bs, N_MEM, WIDTH),
                        cache_mem_v.reshape(bs, N_MEM, WIDTH))
    h_s = _merge(xs2, s5_s, cv_s, xa_s.reshape(n_s, WIDTH).astype(BF), *merge_w)
    y_sample = _peer_final(h_s, *peer_w).reshape(bs, ts, D_MODEL)
    conv_s = jnp.concatenate([st_conv[:, ts:], jnp.swapaxes(a_s.reshape(ts, bs, WIDTH), 0, 1)], axis=1)

    st_shape_p = (1, bp, S5_GROUPS, S5_STATE)
    st_shape_s = (1, bs, S5_GROUPS, S5_STATE)
    return (y_prompt, y_sample, re_p.reshape(st_shape_p), im_p.reshape(st_shape_p), conv_p[None],
            mk.reshape(1, bp, N_MEM, XA_HEADS, XA_HEAD_DIM), mv.reshape(1, bp, N_MEM, XA_HEADS, XA_HEAD_DIM),
            re_s.reshape(st_shape_s), im_s.reshape(st_shape_s), conv_s[None])
```

```python
import functools
import math

import jax
import jax.numpy as jnp
from jax import lax
from jax.experimental import pallas as pl
from jax.experimental.pallas import tpu as pltpu

F32 = jnp.float32
BF = jnp.bfloat16

D_MODEL = 1024
WIDTH = 512
S5_GROUPS = 32
S5_GROUP = 16
S5_STATE = 64
S5_CH = S5_GROUPS * S5_STATE
S5_SUPER = 2
CONV_K = 31
HIST = CONV_K - 1
N_MEM = 256
XA_HEADS = 4
XA_HEAD_DIM = 128
PEER_HEADS = 8
PEER_NKEYS = 128
PEER_N_EXPERTS = PEER_NKEYS * PEER_NKEYS
PEER_TOPK = 16
RMS_EPS = 1e-6
LN_EPS = 1e-5

ROW_TILE = 512
PEER_TOKENS = 512
PEER_EXPERTS = 2048
PEER_CHUNK = 512
PEER_ROWS = 16
PEER_LANES = 256
PEER_ACT_LANES = 512
PEER_VROWS = 1024
VMEM_LIMIT = 56 * 1024 * 1024

_NT = (((1,), (1,)), ((), ()))


def _params(sem):
    return pltpu.CompilerParams(dimension_semantics=sem, vmem_limit_bytes=VMEM_LIMIT)


def _rms(x, g):
    return x * lax.rsqrt(jnp.mean(x * x, axis=-1, keepdims=True) + RMS_EPS) * g


def _sigmoid(x):
    return 1.0 / (1.0 + jnp.exp(-x))


def _gelu(x):
    return 0.5 * x * (1.0 + lax.erf(x * (1.0 / math.sqrt(2.0))))


def _const(shape):
    nd = len(shape)
    return pl.BlockSpec(shape, lambda *_: (0,) * nd, pipeline_mode=pl.Buffered(1))


def _memkv_kernel(m_ref, g_ref, wk_ref, wv_ref, k_ref, v_ref):
    m = _rms(m_ref[...], g_ref[...]).astype(BF)
    k_ref[...] = jnp.dot(m, wk_ref[...], preferred_element_type=F32)
    v_ref[...] = jnp.dot(m, wv_ref[...], preferred_element_type=F32)


def _memkv(mem2d, g, wk, wv):
    rows = mem2d.shape[0]
    return pl.pallas_call(
        _memkv_kernel,
        grid=(rows // ROW_TILE,),
        in_specs=[pl.BlockSpec((ROW_TILE, D_MODEL), lambda i: (i, 0)), _const((1, D_MODEL)),
                  _const((D_MODEL, WIDTH)), _const((D_MODEL, WIDTH))],
        out_specs=[pl.BlockSpec((ROW_TILE, WIDTH), lambda i: (i, 0))] * 2,
        out_shape=[jax.ShapeDtypeStruct((rows, WIDTH), F32)] * 2,
        compiler_params=_params(("parallel",)),
        name="mem_kv",
    )(mem2d, g, wk, wv)


def _seq_kernel(nb, tt, cw, n_steps,
                x_ref, g1_ref, w1_ref, b1_ref, p_ref, pt_ref, bmat_ref, cmat_ref, dsk_ref, ldt_ref, are_ref, aim_ref,
                h0r_ref, h0i_ref, hist_ref, dww_ref, dwb_ref, lng_ref, lnb_ref, gluw_ref, glub_ref,
                q_ref, s5_ref, co_ref, a_ref, sre_ref, sim_ref,
                bu_scr, st_scr, par_scr, cb_scr, y_scr):
    i = pl.program_id(0)
    rt = nb * tt
    half = S5_CH // S5_SUPER
    uw = WIDTH // S5_SUPER

    @pl.when(i == 0)
    def _():
        st_scr[0] = h0r_ref[...]
        st_scr[1] = h0i_ref[...]
        cb_scr[0:HIST * nb, :] = hist_ref[...]

    dt = jnp.exp(ldt_ref[...])
    are = are_ref[...]
    aim = aim_ref[...]
    mag = jnp.exp(are * dt)
    lr = mag * jnp.cos(aim * dt)
    li = mag * jnp.sin(aim * dt)
    den = are * are + aim * aim
    nr = lr - 1.0
    par_scr[0:1, :] = lr
    par_scr[1:2, :] = li
    par_scr[2:3, :] = (nr * are + li * aim) / den
    par_scr[3:4, :] = (li * are - nr * aim) / den

    n_bm = _rms(x_ref[...].reshape(rt, D_MODEL), g1_ref[...]).astype(BF)
    n_tm = jnp.dot(p_ref[...], n_bm, preferred_element_type=F32).astype(BF)
    z = jnp.dot(n_tm, w1_ref[...], preferred_element_type=F32) + b1_ref[...]
    u = z[:, :WIDTH]
    a = z[:, WIDTH:2 * WIDTH] * _sigmoid(z[:, 2 * WIDTH:3 * WIDTH])
    q_tm = z[:, 3 * WIDTH:4 * WIDTH].astype(BF)
    q_ref[...] = jnp.dot(pt_ref[...], q_tm, preferred_element_type=F32).astype(BF).reshape(q_ref.shape)
    a_ref[...] = a

    u_bf = u.astype(BF)
    for s in range(S5_SUPER):
        bu_scr[:, 2 * half * s:2 * half * (s + 1)] = jnp.dot(u_bf[:, uw * s:uw * (s + 1)], bmat_ref[s],
                                                             preferred_element_type=F32)

    for s in range(S5_SUPER):
        for c in range(half // cw):
            nat = slice(half * s + c * cw, half * s + (c + 1) * cw)
            re_sl = slice(2 * half * s + c * cw, 2 * half * s + (c + 1) * cw)
            im_sl = slice(2 * half * s + half + c * cw, 2 * half * s + half + (c + 1) * cw)
            lr_b = jnp.broadcast_to(par_scr[0:1, nat], (nb, cw))
            li_b = jnp.broadcast_to(par_scr[1:2, nat], (nb, cw))
            cr_b = jnp.broadcast_to(par_scr[2:3, nat], (nb, cw))
            ci_b = jnp.broadcast_to(par_scr[3:4, nat], (nb, cw))

            def step(t, carry, re_sl=re_sl, im_sl=im_sl, lr_b=lr_b, li_b=li_b, cr_b=cr_b, ci_b=ci_b):
                s_re, s_im = carry
                r0 = pl.multiple_of(t * nb, nb)
                rr = bu_scr[pl.ds(r0, nb), re_sl]
                ri = bu_scr[pl.ds(r0, nb), im_sl]
                n_re = lr_b * s_re - li_b * s_im + (cr_b * rr - ci_b * ri)
                n_im = lr_b * s_im + li_b * s_re + (cr_b * ri + ci_b * rr)
                bu_scr[pl.ds(r0, nb), re_sl] = n_re
                bu_scr[pl.ds(r0, nb), im_sl] = n_im
                return n_re, n_im

            s_re, s_im = lax.fori_loop(0, tt, step, (st_scr[0, :, nat], st_scr[1, :, nat]), unroll=min(tt, 8))
            st_scr[0, :, nat] = s_re
            st_scr[1, :, nat] = s_im

    y = jnp.concatenate(
        [jnp.dot(bu_scr[:, 2 * half * s:2 * half * (s + 1)].astype(BF), cmat_ref[s], preferred_element_type=F32)
         for s in range(S5_SUPER)], axis=-1) + dsk_ref[...] * u
    zz = _gelu(y)
    gate = _sigmoid(jnp.dot(zz.astype(BF), gluw_ref[...], preferred_element_type=F32) + glub_ref[...])
    s5_tm = (zz * gate).astype(BF)
    s5_ref[...] = jnp.dot(pt_ref[...], s5_tm, preferred_element_type=F32).astype(BF).reshape(s5_ref.shape)

    cb_scr[HIST * nb:HIST * nb + rt, :] = a
    rb = 32

    def cblock(bi, carry):
        r0 = pl.multiple_of(bi * rb, rb)
        acc = jnp.zeros((rb, WIDTH), F32)
        for k in range(CONV_K):
            acc = acc + dww_ref[k:k + 1, :] * cb_scr[pl.ds(r0 + k * nb, rb), :]
        y_scr[pl.ds(r0, rb), :] = acc
        return carry

    lax.fori_loop(0, rt // rb, cblock, 0)
    if n_steps > 1:
        cb_scr[0:HIST * nb, :] = cb_scr[rt:rt + HIST * nb, :]

    yv = y_scr[...] + dwb_ref[...]
    yc = yv - jnp.mean(yv, axis=-1, keepdims=True)
    var = jnp.mean(yc * yc, axis=-1, keepdims=True)
    ln = yc * lax.rsqrt(var + LN_EPS) * lng_ref[...] + lnb_ref[...]
    co_tm = (ln * _sigmoid(ln)).astype(BF)
    co_ref[...] = jnp.dot(pt_ref[...], co_tm, preferred_element_type=F32).astype(BF).reshape(co_ref.shape)

    @pl.when(i == n_steps - 1)
    def _():
        sre_ref[...] = st_scr[0]
        sim_ref[...] = st_scr[1]


def _seq(x, nb, tt, cw, h0r, h0i, hist, g1, w1, b1, sw):
    rt = nb * tt
    if x.ndim == 3:
        n_steps = x.shape[1] // tt
        xspec = pl.BlockSpec((nb, tt, D_MODEL), lambda i: (0, i, 0))
        ospec = pl.BlockSpec((nb, tt, WIDTH), lambda i: (0, i, 0))
        oshape = jax.ShapeDtypeStruct((nb, x.shape[1], WIDTH), BF)
    else:
        n_steps = 1
        xspec = pl.BlockSpec((rt, D_MODEL), lambda i: (0, 0))
        ospec = pl.BlockSpec((rt, WIDTH), lambda i: (0, 0))
        oshape = jax.ShapeDtypeStruct((rt, WIDTH), BF)
    col = jnp.arange(rt)
    perm = (jnp.arange(rt)[:, None] == ((col % tt) * nb + col // tt)[None, :]).astype(BF)
    st = _const((nb, S5_CH))
    vec = _const((1, WIDTH))
    chv = _const((1, S5_CH))
    half = S5_CH // S5_SUPER
    return pl.pallas_call(
        functools.partial(_seq_kernel, nb, tt, cw, n_steps),
        grid=(n_steps,),
        in_specs=[xspec, _const((1, D_MODEL)), _const((D_MODEL, 4 * WIDTH)), _const((1, 4 * WIDTH)),
                  _const((rt, rt)), _const((rt, rt)),
                  _const((S5_SUPER, WIDTH // S5_SUPER, 2 * half)), _const((S5_SUPER, 2 * half, WIDTH // S5_SUPER)),
                  vec, chv, chv, chv, st, st, _const((HIST * nb, WIDTH)), _const((CONV_K, WIDTH)), vec, vec, vec,
                  _const((WIDTH, WIDTH)), vec],
        out_specs=[ospec, ospec, ospec, _const((rt, WIDTH)), st, st],
        out_shape=[oshape, oshape, oshape, jax.ShapeDtypeStruct((rt, WIDTH), F32),
                   jax.ShapeDtypeStruct((nb, S5_CH), F32), jax.ShapeDtypeStruct((nb, S5_CH), F32)],
        scratch_shapes=[pltpu.VMEM((rt, 2 * S5_CH), F32), pltpu.VMEM((2, nb, S5_CH), F32),
                        pltpu.VMEM((8, S5_CH), F32), pltpu.VMEM(((HIST + tt) * nb, WIDTH), F32),
                        pltpu.VMEM((rt, WIDTH), F32)],
        compiler_params=_params(("arbitrary",)),
        name="s5_conv",
    )(x, g1, w1, b1, perm, perm.T, sw["bmat"], sw["cmat"], sw["dskip"], sw["ldt"], sw["are"], sw["aim"],
      h0r, h0i, hist, sw["dww"], sw["dwb"], sw["lng"], sw["lnb"], sw["gluw"], sw["glub"])


def _attn_heads(q, k, v):
    scale = XA_HEAD_DIM ** -0.5
    outs = []
    for h in range(XA_HEADS):
        sl = slice(h * XA_HEAD_DIM, (h + 1) * XA_HEAD_DIM)
        s = lax.dot_general(q[:, sl], k[:, sl], _NT, preferred_element_type=F32) * scale
        e = jnp.exp(s - jnp.max(s, axis=-1, keepdims=True))
        p = e / jnp.sum(e, axis=-1, keepdims=True)
        outs.append(jnp.dot(p.astype(BF), v[:, sl], preferred_element_type=F32))
    return jnp.concatenate(outs, axis=-1)


def _attn_prompt_kernel(q_ref, k_ref, v_ref, o_ref):
    o_ref[...] = _attn_heads(q_ref[...], k_ref[...].astype(BF), v_ref[...].astype(BF)).astype(o_ref.dtype)


def _attn_prompt(q3, k3, v3):
    bn, tn, _ = q3.shape
    tm = min(ROW_TILE, tn)
    kv = pl.BlockSpec((None, N_MEM, WIDTH), lambda bi, i: (bi, 0, 0))
    qs = pl.BlockSpec((None, tm, WIDTH), lambda bi, i: (bi, i, 0))
    return pl.pallas_call(
        _attn_prompt_kernel,
        grid=(bn, tn // tm),
        in_specs=[qs, kv, kv],
        out_specs=qs,
        out_shape=jax.ShapeDtypeStruct((bn, tn, WIDTH), BF),
        compiler_params=_params(("parallel", "parallel")),
        name="xattn_prompt",
    )(q3, k3, v3)


def _attn_sample_kernel(bb, tq, q_ref, k_ref, v_ref, o_ref):
    nr = XA_HEADS * tq
    nc = N_MEM * XA_HEADS
    row_head = lax.broadcasted_iota(jnp.int32, (nr, nc), 0) // tq
    col_head = lax.broadcasted_iota(jnp.int32, (nr, nc), 1) % XA_HEADS
    own = row_head == col_head
    scale = XA_HEAD_DIM ** -0.5
    for b in range(bb):
        qb = q_ref[b]
        qrows = jnp.concatenate([qb[:, h * XA_HEAD_DIM:(h + 1) * XA_HEAD_DIM] for h in range(XA_HEADS)],
                                axis=0).astype(BF)
        s = lax.dot_general(qrows, k_ref[b].astype(BF), _NT, preferred_element_type=F32) * scale
        s = jnp.where(own, s, -jnp.inf)
        e = jnp.exp(s - jnp.max(s, axis=-1, keepdims=True))
        p = e / jnp.sum(e, axis=-1, keepdims=True)
        of = jnp.dot(p.astype(BF), v_ref[b].astype(BF), preferred_element_type=F32)
        o_ref[b] = jnp.concatenate([of[h * tq:(h + 1) * tq] for h in range(XA_HEADS)], axis=1)


def _attn_sample(q3, k3, v3):
    bn, tq, _ = q3.shape
    bb = 8
    kv = pl.BlockSpec((bb, N_MEM * XA_HEADS, XA_HEAD_DIM), lambda i: (i, 0, 0))
    qs = pl.BlockSpec((bb, tq, WIDTH), lambda i: (i, 0, 0))
    return pl.pallas_call(
        functools.partial(_attn_sample_kernel, bb, tq),
        grid=(bn // bb,),
        in_specs=[qs, kv, kv],
        out_specs=qs,
        out_shape=jax.ShapeDtypeStruct((bn, tq, WIDTH), F32),
        compiler_params=_params(("parallel",)),
        name="xattn_sample",
    )(q3, k3, v3)


def _merge_kernel(x_ref, s5_ref, cv_ref, xa_ref, g_ref, wg_ref, bg_ref, ws5_ref, wcv_ref, wxa_ref, wout_ref, h_ref):
    x = x_ref[...]
    n = _rms(x, g_ref[...]).astype(BF)
    gate = _sigmoid(jnp.dot(n, wg_ref[...], preferred_element_type=F32) + bg_ref[...])
    merged = (gate[:, :D_MODEL] * jnp.dot(s5_ref[...], ws5_ref[...], preferred_element_type=F32)
              + gate[:, D_MODEL:2 * D_MODEL] * jnp.dot(cv_ref[...], wcv_ref[...], preferred_element_type=F32)
              + gate[:, 2 * D_MODEL:] * jnp.dot(xa_ref[...], wxa_ref[...], preferred_element_type=F32))
    h_ref[...] = x + jnp.dot(merged.astype(BF), wout_ref[...], preferred_element_type=F32)


def _merge(x2, s5, cv, xa, g, wg, bg, ws5, wcv, wxa, wout):
    rows = x2.shape[0]
    xs = pl.BlockSpec((ROW_TILE, D_MODEL), lambda i: (i, 0))
    br = pl.BlockSpec((ROW_TILE, WIDTH), lambda i: (i, 0))
    proj = _const((WIDTH, D_MODEL))
    return pl.pallas_call(
        _merge_kernel,
        grid=(rows // ROW_TILE,),
        in_specs=[xs, br, br, br, _const((1, D_MODEL)), _const((D_MODEL, 3 * D_MODEL)), _const((1, 3 * D_MODEL)),
                  proj, proj, proj, _const((D_MODEL, D_MODEL))],
        out_specs=xs,
        out_shape=jax.ShapeDtypeStruct((rows, D_MODEL), F32),
        compiler_params=_params(("parallel",)),
        name="merge_out",
    )(x2, s5, cv, xa, g, wg, bg, ws5, wcv, wxa, wout)


def _top_values(work, count):
    vals = []
    for _ in range(count):
        m = jnp.max(work, axis=0, keepdims=True)
        vals.append(m)
        work = jnp.where(work == m, -jnp.inf, work)
    return vals


def _sorting_network(n):
    pairs = []

    def merge(lo, cnt, r):
        step = 2 * r
        if step < cnt:
            merge(lo, cnt, step)
            merge(lo + r, cnt, step)
            pairs.extend((k, k + r) for k in range(lo + r, lo + cnt - r, step))
        else:
            pairs.append((lo, lo + r))

    def sort(lo, cnt):
        if cnt > 1:
            sort(lo, cnt // 2)
            sort(lo + cnt // 2, cnt // 2)
            merge(lo, cnt, 1)

    sort(0, n)
    return pairs


def _top_values_sorted(work, count):
    cols = [work[8 * k:8 * (k + 1)] for k in range(work.shape[0] // 8)]
    for lo, hi in _sorting_network(len(cols)):
        cols[lo], cols[hi] = jnp.maximum(cols[lo], cols[hi]), jnp.minimum(cols[lo], cols[hi])
    vals = []
    for k in range(count):
        m = jnp.max(cols[0], axis=0, keepdims=True)
        vals.append(m)
        needed = count - 1 - k
        if needed == 0:
            break
        hit = cols[0] == m
        cols = [jnp.where(hit, cols[d + 1] if d + 1 < len(cols) else -jnp.inf, cols[d])
                for d in range(min(len(cols), needed))]
    return vals


def _peer_kernel(n_esteps, h_ref, g2_ref, wqt_ref, keys_ref, u_ref, vt_ref, gf_ref, y_ref,
                 x2_scr, qt_scr, st_scr, sv_scr, e1_scr, e2_scr, cz_scr, at_scr, mt_scr, acc_scr):
    j = pl.program_id(1)
    tn = PEER_TOKENS
    nk = PEER_TOPK + 1
    lanes = 128

    @pl.when(j == 0)
    def _():
        x2 = _rms(h_ref[...], g2_ref[...]).astype(BF)
        x2_scr[...] = x2
        qt_scr[...] = lax.dot_general(wqt_ref[...], x2, _NT, preferred_element_type=F32)
        for hp in range(2 * PEER_HEADS):
            st_scr[hp] = jnp.dot(keys_ref[hp % 2], qt_scr[hp * PEER_NKEYS:(hp + 1) * PEER_NKEYS, :].astype(BF),
                                 preferred_element_type=F32)

        def half_body(hp, carry):
            for lt in range(tn // lanes):
                ls = slice(lt * lanes, (lt + 1) * lanes)
                vals = _top_values_sorted(st_scr[hp, :, ls], nk)
                for k in range(nk):
                    sv_scr[hp, k:k + 1, ls] = vals[k]
                sv_scr[hp, nk:24, ls] = jnp.full((24 - nk, lanes), -jnp.inf, F32)
            return carry

        lax.fori_loop(0, 2 * PEER_HEADS, half_body, 0)

        row8 = lax.broadcasted_iota(jnp.int32, (8, lanes), 0)

        def head_body(h, carry):
            for lt in range(tn // lanes):
                ls = slice(lt * lanes, (lt + 1) * lanes)
                sv1 = sv_scr[2 * h, :, ls]
                sv2 = sv_scr[2 * h + 1, :, ls]
                cands = [sv1[0:1] + sv2, sv1 + sv2[0:1]]
                r_max = math.isqrt(nk)
                for r in range(2, r_max + 1):
                    cands.append(jnp.where(row8 < nk // r, sv1[r - 1:r] + sv2[0:8], -jnp.inf))
                for jj in range(2, nk // (r_max + 1) + 1):
                    cands.append(jnp.where(row8 < nk // jj, sv1[0:8] + sv2[jj - 1:jj], -jnp.inf))
                cv = _top_values(jnp.concatenate(cands, axis=0), nk)
                z = jnp.ones_like(cv[0])
                for k in range(1, PEER_TOPK):
                    z = z + jnp.exp(cv[k] - cv[0])
                tau = 0.5 * (cv[PEER_TOPK - 1] + cv[PEER_TOPK])
                hz = 0.5 / z
                cz_scr[h, :, ls] = jnp.broadcast_to(jnp.exp(tau - cv[0]) * hz, (8, lanes))
                e1_scr[h, :, ls] = jnp.exp(st_scr[2 * h, :, ls] - sv1[0:1]) * hz
                e2_scr[h, :, ls] = jnp.exp(st_scr[2 * h + 1, :, ls] - sv2[0:1])
            return carry

        lax.fori_loop(0, PEER_HEADS, head_body, 0)
        acc_scr[...] = jnp.zeros_like(acc_scr)

    ch = PEER_CHUNK
    n_chunks = PEER_EXPERTS // ch

    def act_piece(c, nh):
        ns = slice(nh * PEER_ACT_LANES, (nh + 1) * PEER_ACT_LANES)
        at_scr[c % 2, :, ns] = lax.dot_general(u_ref[c * ch:(c + 1) * ch, :], x2_scr[ns, :], _NT,
                                               preferred_element_type=F32)

    def value_piece(c, mq):
        ms = slice(mq * PEER_VROWS, (mq + 1) * PEER_VROWS)
        acc_scr[ms, :] += jnp.dot(vt_ref[ms, c * ch:(c + 1) * ch], mt_scr[c % 2], preferred_element_type=F32)

    def gate_blocks(c):
        blocks = []
        for a in range(ch // PEER_NKEYS):
            for lw in range(tn // PEER_LANES):
                shared = {}

                def block(rb, a=a, lw=lw, shared=shared):
                    ls = slice(lw * PEER_LANES, (lw + 1) * PEER_LANES)
                    if not shared:
                        i1 = j * (PEER_EXPERTS // PEER_NKEYS) + (c * ch) // PEER_NKEYS + a
                        shared["e1"] = [jnp.broadcast_to(e1_scr[h, pl.ds(i1, 1), ls], (PEER_ROWS, PEER_LANES))
                                        for h in range(PEER_HEADS)]
                        shared["cz"] = [jnp.concatenate([cz_scr[h, :, ls]] * (PEER_ROWS // 8), axis=0)
                                        for h in range(PEER_HEADS)]
                    i2 = rb * PEER_ROWS
                    r0 = a * PEER_NKEYS + i2
                    xa = at_scr[c % 2, r0:r0 + PEER_ROWS, ls]
                    act = xa + xa * lax.erf(xa * (1.0 / math.sqrt(2.0)))
                    w = jnp.zeros((PEER_ROWS, PEER_LANES), F32)
                    for h in range(PEER_HEADS):
                        p = shared["e1"][h] * e2_scr[h, i2:i2 + PEER_ROWS, ls]
                        w = w + jnp.where(p >= shared["cz"][h], p, 0.0)
                    mt_scr[c % 2, r0:r0 + PEER_ROWS, ls] = (w * act).astype(BF)

                blocks += [functools.partial(block, rb) for rb in range(PEER_NKEYS // PEER_ROWS)]
        return blocks

    for nh in range(tn // PEER_ACT_LANES):
        act_piece(0, nh)
    for c in range(n_chunks):
        pieces = []
        if c + 1 < n_chunks:
            pieces += [functools.partial(act_piece, c + 1, nh) for nh in range(tn // PEER_ACT_LANES)]
        if c >= 1:
            pieces += [functools.partial(value_piece, c - 1, mq) for mq in range(D_MODEL // PEER_VROWS)]
        blocks = gate_blocks(c)
        for bi, blk in enumerate(blocks):
            for pi, piece in enumerate(pieces):
                if (pi * len(blocks)) // len(pieces) == bi:
                    piece()
            blk()
    for mq in range(D_MODEL // PEER_VROWS):
        value_piece(n_chunks - 1, mq)

    @pl.when(j == n_esteps - 1)
    def _():
        h2 = h_ref[...] + acc_scr[...].T
        y_ref[...] = _rms(h2, gf_ref[...])


def _peer_final(h2d, g2, wqt, keys, u_bf, vt_bf, gf):
    rows = h2d.shape[0]
    tn = PEER_TOKENS
    n_esteps = PEER_N_EXPERTS // PEER_EXPERTS
    hs = pl.BlockSpec((tn, D_MODEL), lambda i, j: (i, 0))
    return pl.pallas_call(
        functools.partial(_peer_kernel, n_esteps),
        grid=(rows // tn, n_esteps),
        in_specs=[hs, _const((1, D_MODEL)), _const((2 * PEER_HEADS * PEER_NKEYS, D_MODEL)),
                  _const((2, PEER_NKEYS, PEER_NKEYS)),
                  pl.BlockSpec((PEER_EXPERTS, D_MODEL), lambda i, j: (j, 0)),
                  pl.BlockSpec((D_MODEL, PEER_EXPERTS), lambda i, j: (0, j)), _const((1, D_MODEL))],
        out_specs=hs,
        out_shape=jax.ShapeDtypeStruct((rows, D_MODEL), F32),
        scratch_shapes=[pltpu.VMEM((tn, D_MODEL), BF),
                        pltpu.VMEM((2 * PEER_HEADS * PEER_NKEYS, tn), F32),
                        pltpu.VMEM((2 * PEER_HEADS, PEER_NKEYS, tn), F32),
                        pltpu.VMEM((2 * PEER_HEADS, 24, tn), F32),
                        pltpu.VMEM((PEER_HEADS, PEER_NKEYS, tn), F32),
                        pltpu.VMEM((PEER_HEADS, PEER_NKEYS, tn), F32),
                        pltpu.VMEM((PEER_HEADS, 8, tn), F32),
                        pltpu.VMEM((2, PEER_CHUNK, tn), F32),
                        pltpu.VMEM((2, PEER_CHUNK, tn), BF),
                        pltpu.VMEM((D_MODEL, tn), F32)],
        compiler_params=_params(("parallel", "arbitrary")),
        name="peer_final",
    )(h2d, g2, wqt, keys, u_bf, vt_bf, gf)


def _block_diag(w):
    g, r, c = w.shape
    idx = jnp.arange(g)
    return jnp.zeros((g, r, g, c), w.dtype).at[idx, :, idx, :].set(w).reshape(g * r, g * c)


def _row(v):
    return v.reshape(1, -1).astype(F32)


def kernel(x_prompt, x_sample, mem_prompt, state_ssm_re, state_ssm_im, state_conv, cache_mem_k, cache_mem_v,
           norm1_g, w_in, b_in, s5_log_dt, s5_a_re, s5_a_im, s5_b_re, s5_b_im, s5_c_re, s5_c_im, s5_d,
           s5_glu_w, s5_glu_b, w_s5_proj, conv_dw_w, conv_dw_b, conv_ln_g, conv_ln_b, w_conv_proj, mem_norm_g,
           w_mem_k, w_mem_v, w_xa_proj, w_out, norm2_g, peer_w_q, peer_sub_keys, peer_u, peer_v, final_norm_g):
    assert norm1_g.shape[0] == 1, "single-layer trunk"
    bp, tp, _ = x_prompt.shape
    bs, ts, _ = x_sample.shape
    n_s = bs * ts
    gps = S5_GROUPS // S5_SUPER

    w_in2 = w_in.reshape(D_MODEL, -1)
    b_in1 = b_in.reshape(-1)
    w1 = w_in2[:, :4 * WIDTH].astype(BF)
    b1 = _row(b_in1[:4 * WIDTH])
    wg = w_in2[:, 4 * WIDTH:].astype(BF)
    bg = _row(b_in1[4 * WIDTH:])

    def per_super(w):
        return jnp.stack([_block_diag(w[s * gps:(s + 1) * gps]) for s in range(S5_SUPER)])

    b_re_t = jnp.swapaxes(s5_b_re.reshape(S5_GROUPS, S5_STATE, S5_GROUP), 1, 2)
    b_im_t = jnp.swapaxes(s5_b_im.reshape(S5_GROUPS, S5_STATE, S5_GROUP), 1, 2)
    c_re_t = jnp.swapaxes(s5_c_re.reshape(S5_GROUPS, S5_GROUP, S5_STATE), 1, 2)
    c_im_t = jnp.swapaxes(s5_c_im.reshape(S5_GROUPS, S5_GROUP, S5_STATE), 1, 2)
    sw = {
        "bmat": jnp.concatenate([per_super(b_re_t), per_super(b_im_t)], axis=2).astype(BF),
        "cmat": jnp.concatenate([per_super(c_re_t), -per_super(c_im_t)], axis=1).astype(BF),
        "dskip": _row(s5_d),
        "ldt": _row(jnp.repeat(s5_log_dt.reshape(-1), S5_STATE)),
        "are": _row(s5_a_re),
        "aim": _row(s5_a_im),
        "dww": conv_dw_w.reshape(CONV_K, WIDTH).astype(F32),
        "dwb": _row(conv_dw_b),
        "lng": _row(conv_ln_g),
        "lnb": _row(conv_ln_b),
        "gluw": s5_glu_w.reshape(WIDTH, WIDTH).astype(BF),
        "glub": _row(s5_glu_b),
    }
    g1 = _row(norm1_g)
    merge_w = (g1, wg, bg, w_s5_proj.reshape(WIDTH, D_MODEL).astype(BF), w_conv_proj.reshape(WIDTH, D_MODEL).astype(BF),
               w_xa_proj.reshape(WIDTH, D_MODEL).astype(BF), w_out.reshape(D_MODEL, D_MODEL).astype(BF))
    peer_w = (_row(norm2_g), peer_w_q.reshape(D_MODEL, -1).T.astype(BF),
              peer_sub_keys.reshape(2, PEER_NKEYS, PEER_NKEYS).astype(BF),
              peer_u.reshape(PEER_N_EXPERTS, D_MODEL).astype(BF),
              peer_v.reshape(PEER_N_EXPERTS, D_MODEL).T.astype(BF), _row(final_norm_g))

    mk, mv = _memkv(mem_prompt.reshape(bp * N_MEM, D_MODEL), _row(mem_norm_g),
                    w_mem_k.reshape(D_MODEL, WIDTH).astype(BF), w_mem_v.reshape(D_MODEL, WIDTH).astype(BF))
    tt_p = 64
    zeros_st = jnp.zeros((bp, S5_CH), F32)
    q_p, s5_p, cv_p, a_last, re_p, im_p = _seq(x_prompt, bp, tt_p, 512, zeros_st, zeros_st,
                                               jnp.zeros((HIST * bp, WIDTH), F32), g1, w1, b1, sw)
    xa_p = _attn_prompt(q_p, mk.reshape(bp, N_MEM, WIDTH), mv.reshape(bp, N_MEM, WIDTH))
    h_p = _merge(x_prompt.reshape(bp * tp, D_MODEL), s5_p.reshape(bp * tp, WIDTH), cv_p.reshape(bp * tp, WIDTH),
                 xa_p.reshape(bp * tp, WIDTH), *merge_w)
    y_prompt = _peer_final(h_p, *peer_w).reshape(bp, tp, D_MODEL)
    conv_p = jnp.swapaxes(a_last[(tt_p - HIST) * bp:].reshape(HIST, bp, WIDTH), 0, 1)

    xs2 = x_sample.reshape(n_s, D_MODEL)
    st_conv = state_conv.reshape(bs, HIST, WIDTH)
    hist_s = jnp.swapaxes(st_conv, 0, 1).reshape(HIST * bs, WIDTH)
    q_s, s5_s, cv_s, a_s, re_s, im_s = _seq(xs2, bs, ts, 128, state_ssm_re.reshape(bs, S5_CH),
                                            state_ssm_im.reshape(bs, S5_CH), hist_s, g1, w1, b1, sw)
    xa_s = _attn_sample(q_s.reshape(bs, ts, WIDTH).astype(F32),
                        cache_mem_k.reshape(bs, N_MEM * XA_HEADS, XA_HEAD_DIM),
                        cache_mem_v.reshape(bs, N_MEM * XA_HEADS, XA_HEAD_DIM))
    h_s = _merge(xs2, s5_s, cv_s, xa_s.reshape(n_s, WIDTH).astype(BF), *merge_w)
    y_sample = _peer_final(h_s, *peer_w).reshape(bs, ts, D_MODEL)
    conv_s = jnp.concatenate([st_conv[:, ts:], jnp.swapaxes(a_s.reshape(ts, bs, WIDTH), 0, 1)], axis=1)

    st_shape_p = (1, bp, S5_GROUPS, S5_STATE)
    st_shape_s = (1, bs, S5_GROUPS, S5_STATE)
    return (y_prompt, y_sample, re_p.reshape(st_shape_p), im_p.reshape(st_shape_p), conv_p[None],
            mk.reshape(1, bp, N_MEM, XA_HEADS, XA_HEAD_DIM), mv.reshape(1, bp, N_MEM, XA_HEADS, XA_HEAD_DIM),
            re_s.reshape(st_shape_s), im_s.reshape(st_shape_s), conv_s[None])
```

```python
import functools
import math

import jax
import jax.numpy as jnp
from jax import lax
from jax.experimental import pallas as pl
from jax.experimental.pallas import tpu as pltpu

F32 = jnp.float32
BF = jnp.bfloat16

D_MODEL = 1024
WIDTH = 512
S5_GROUPS = 32
S5_GROUP = 16
S5_STATE = 64
S5_CH = S5_GROUPS * S5_STATE
S5_SUPER = 2
CONV_K = 31
HIST = CONV_K - 1
N_MEM = 256
XA_HEADS = 4
XA_HEAD_DIM = 128
PEER_HEADS = 8
PEER_NKEYS = 128
PEER_N_EXPERTS = PEER_NKEYS * PEER_NKEYS
PEER_TOPK = 16
RMS_EPS = 1e-6
LN_EPS = 1e-5

ROW_TILE = 512
PEER_TOKENS = 512
PEER_EXPERTS = 2048
PEER_CHUNK = 512
PEER_ROWS = 16
PEER_LANES = 256
PEER_ACT_LANES = 512
PEER_VROWS = 1024
VMEM_LIMIT = 56 * 1024 * 1024

_NT = (((1,), (1,)), ((), ()))


def _params(sem):
    return pltpu.CompilerParams(dimension_semantics=sem, vmem_limit_bytes=VMEM_LIMIT)


def _rms(x, g):
    return x * lax.rsqrt(jnp.mean(x * x, axis=-1, keepdims=True) + RMS_EPS) * g


def _sigmoid(x):
    return 1.0 / (1.0 + jnp.exp(-x))


def _gelu(x):
    return 0.5 * x * (1.0 + lax.erf(x * (1.0 / math.sqrt(2.0))))


def _const(shape):
    nd = len(shape)
    return pl.BlockSpec(shape, lambda *_: (0,) * nd, pipeline_mode=pl.Buffered(1))


def _memkv_kernel(m_ref, g_ref, wk_ref, wv_ref, k_ref, v_ref):
    m = _rms(m_ref[...], g_ref[...]).astype(BF)
    k_ref[...] = jnp.dot(m, wk_ref[...], preferred_element_type=F32)
    v_ref[...] = jnp.dot(m, wv_ref[...], preferred_element_type=F32)


def _memkv(mem2d, g, wk, wv):
    rows = mem2d.shape[0]
    return pl.pallas_call(
        _memkv_kernel,
        grid=(rows // ROW_TILE,),
        in_specs=[pl.BlockSpec((ROW_TILE, D_MODEL), lambda i: (i, 0)), _const((1, D_MODEL)),
                  _const((D_MODEL, WIDTH)), _const((D_MODEL, WIDTH))],
        out_specs=[pl.BlockSpec((ROW_TILE, WIDTH), lambda i: (i, 0))] * 2,
        out_shape=[jax.ShapeDtypeStruct((rows, WIDTH), F32)] * 2,
        compiler_params=_params(("parallel",)),
        name="mem_kv",
    )(mem2d, g, wk, wv)


def _seq_kernel(nb, tt, cw, n_steps,
                x_ref, g1_ref, w1_ref, b1_ref, p_ref, pt_ref, bmat_ref, cmat_ref, dsk_ref, ldt_ref, are_ref, aim_ref,
                h0r_ref, h0i_ref, hist_ref, dww_ref, dwb_ref, lng_ref, lnb_ref, gluw_ref, glub_ref,
                q_ref, s5_ref, co_ref, a_ref, sre_ref, sim_ref,
                bu_scr, st_scr, par_scr, cb_scr, y_scr, u_scr):
    i = pl.program_id(0)
    rt = nb * tt
    half = S5_CH // S5_SUPER
    uw = WIDTH // S5_SUPER

    @pl.when(i == 0)
    def _():
        st_scr[0] = h0r_ref[...]
        st_scr[1] = h0i_ref[...]
        cb_scr[0:HIST * nb, :] = hist_ref[...]

    dt = jnp.exp(ldt_ref[...])
    are = are_ref[...]
    aim = aim_ref[...]
    mag = jnp.exp(are * dt)
    lr = mag * jnp.cos(aim * dt)
    li = mag * jnp.sin(aim * dt)
    den = are * are + aim * aim
    nr = lr - 1.0
    par_scr[0:1, :] = lr
    par_scr[1:2, :] = li
    par_scr[2:3, :] = (nr * are + li * aim) / den
    par_scr[3:4, :] = (li * are - nr * aim) / den

    n_bm = _rms(x_ref[...].reshape(rt, D_MODEL), g1_ref[...]).astype(BF)
    n_tm = jnp.dot(p_ref[...], n_bm, preferred_element_type=F32).astype(BF)

    def in_proj(lo, hi):
        return jnp.dot(n_tm, w1_ref[:, lo:hi], preferred_element_type=F32) + b1_ref[:, lo:hi]

    za = in_proj(WIDTH, 3 * WIDTH)
    a = za[:, :WIDTH] * _sigmoid(za[:, WIDTH:])
    a_ref[...] = a
    cb_scr[HIST * nb:HIST * nb + rt, :] = a

    def proj_u():
        u_scr[...] = in_proj(0, WIDTH)

    def proj_q():
        q_tm = in_proj(3 * WIDTH, 4 * WIDTH).astype(BF)
        q_ref[...] = jnp.dot(pt_ref[...], q_tm, preferred_element_type=F32).astype(BF).reshape(q_ref.shape)

    def proj_b(s):
        bu_scr[:, 2 * half * s:2 * half * (s + 1)] = jnp.dot(u_scr[:, uw * s:uw * (s + 1)].astype(BF), bmat_ref[s],
                                                             preferred_element_type=F32)

    pieces = [proj_u, proj_q] + [functools.partial(proj_b, s) for s in range(S5_SUPER)]
    rb = 32
    n_blocks = rt // rb
    for bi in range(n_blocks):
        for pi, piece in enumerate(pieces):
            if (pi * n_blocks) // len(pieces) == bi:
                piece()
        r0 = bi * rb
        acc = jnp.zeros((rb, WIDTH), F32)
        for k in range(CONV_K):
            acc = acc + dww_ref[k:k + 1, :] * cb_scr[r0 + k * nb:r0 + k * nb + rb, :]
        y_scr[r0:r0 + rb, :] = acc
    if n_steps > 1:
        cb_scr[0:HIST * nb, :] = cb_scr[rt:rt + HIST * nb, :]

    yv = y_scr[...] + dwb_ref[...]
    yc = yv - jnp.mean(yv, axis=-1, keepdims=True)
    var = jnp.mean(yc * yc, axis=-1, keepdims=True)
    ln = yc * lax.rsqrt(var + LN_EPS) * lng_ref[...] + lnb_ref[...]
    co_tm = (ln * _sigmoid(ln)).astype(BF)
    co_ref[...] = jnp.dot(pt_ref[...], co_tm, preferred_element_type=F32).astype(BF).reshape(co_ref.shape)

    for s in range(S5_SUPER):
        for c in range(half // cw):
            nat = slice(half * s + c * cw, half * s + (c + 1) * cw)
            re_sl = slice(2 * half * s + c * cw, 2 * half * s + (c + 1) * cw)
            im_sl = slice(2 * half * s + half + c * cw, 2 * half * s + half + (c + 1) * cw)
            lr_b = jnp.broadcast_to(par_scr[0:1, nat], (nb, cw))
            li_b = jnp.broadcast_to(par_scr[1:2, nat], (nb, cw))
            cr_b = jnp.broadcast_to(par_scr[2:3, nat], (nb, cw))
            ci_b = jnp.broadcast_to(par_scr[3:4, nat], (nb, cw))

            def step(t, carry, re_sl=re_sl, im_sl=im_sl, lr_b=lr_b, li_b=li_b, cr_b=cr_b, ci_b=ci_b):
                s_re, s_im = carry
                r0 = pl.multiple_of(t * nb, nb)
                rr = bu_scr[pl.ds(r0, nb), re_sl]
                ri = bu_scr[pl.ds(r0, nb), im_sl]
                n_re = lr_b * s_re - li_b * s_im + (cr_b * rr - ci_b * ri)
                n_im = lr_b * s_im + li_b * s_re + (cr_b * ri + ci_b * rr)
                bu_scr[pl.ds(r0, nb), re_sl] = n_re
                bu_scr[pl.ds(r0, nb), im_sl] = n_im
                return n_re, n_im

            s_re, s_im = lax.fori_loop(0, tt, step, (st_scr[0, :, nat], st_scr[1, :, nat]), unroll=min(tt, 8))
            st_scr[0, :, nat] = s_re
            st_scr[1, :, nat] = s_im

    y = jnp.concatenate(
        [jnp.dot(bu_scr[:, 2 * half * s:2 * half * (s + 1)].astype(BF), cmat_ref[s], preferred_element_type=F32)
         for s in range(S5_SUPER)], axis=-1) + dsk_ref[...] * u_scr[...]
    zz = _gelu(y)
    gate = _sigmoid(jnp.dot(zz.astype(BF), gluw_ref[...], preferred_element_type=F32) + glub_ref[...])
    s5_tm = (zz * gate).astype(BF)
    s5_ref[...] = jnp.dot(pt_ref[...], s5_tm, preferred_element_type=F32).astype(BF).reshape(s5_ref.shape)

    @pl.when(i == n_steps - 1)
    def _():
        sre_ref[...] = st_scr[0]
        sim_ref[...] = st_scr[1]


def _seq(x, nb, tt, cw, h0r, h0i, hist, g1, w1, b1, sw):
    rt = nb * tt
    if x.ndim == 3:
        n_steps = x.shape[1] // tt
        xspec = pl.BlockSpec((nb, tt, D_MODEL), lambda i: (0, i, 0))
        ospec = pl.BlockSpec((nb, tt, WIDTH), lambda i: (0, i, 0))
        oshape = jax.ShapeDtypeStruct((nb, x.shape[1], WIDTH), BF)
    else:
        n_steps = 1
        xspec = pl.BlockSpec((rt, D_MODEL), lambda i: (0, 0))
        ospec = pl.BlockSpec((rt, WIDTH), lambda i: (0, 0))
        oshape = jax.ShapeDtypeStruct((rt, WIDTH), BF)
    col = jnp.arange(rt)
    perm = (jnp.arange(rt)[:, None] == ((col % tt) * nb + col // tt)[None, :]).astype(BF)
    st = _const((nb, S5_CH))
    vec = _const((1, WIDTH))
    chv = _const((1, S5_CH))
    half = S5_CH // S5_SUPER
    return pl.pallas_call(
        functools.partial(_seq_kernel, nb, tt, cw, n_steps),
        grid=(n_steps,),
        in_specs=[xspec, _const((1, D_MODEL)), _const((D_MODEL, 4 * WIDTH)), _const((1, 4 * WIDTH)),
                  _const((rt, rt)), _const((rt, rt)),
                  _const((S5_SUPER, WIDTH // S5_SUPER, 2 * half)), _const((S5_SUPER, 2 * half, WIDTH // S5_SUPER)),
                  vec, chv, chv, chv, st, st, _const((HIST * nb, WIDTH)), _const((CONV_K, WIDTH)), vec, vec, vec,
                  _const((WIDTH, WIDTH)), vec],
        out_specs=[ospec, ospec, ospec, _const((rt, WIDTH)), st, st],
        out_shape=[oshape, oshape, oshape, jax.ShapeDtypeStruct((rt, WIDTH), F32),
                   jax.ShapeDtypeStruct((nb, S5_CH), F32), jax.ShapeDtypeStruct((nb, S5_CH), F32)],
        scratch_shapes=[pltpu.VMEM((rt, 2 * S5_CH), F32), pltpu.VMEM((2, nb, S5_CH), F32),
                        pltpu.VMEM((8, S5_CH), F32), pltpu.VMEM(((HIST + tt) * nb, WIDTH), F32),
                        pltpu.VMEM((rt, WIDTH), F32), pltpu.VMEM((rt, WIDTH), F32)],
        compiler_params=_params(("arbitrary",)),
        name="s5_conv",
    )(x, g1, w1, b1, perm, perm.T, sw["bmat"], sw["cmat"], sw["dskip"], sw["ldt"], sw["are"], sw["aim"],
      h0r, h0i, hist, sw["dww"], sw["dwb"], sw["lng"], sw["lnb"], sw["gluw"], sw["glub"])


def _attn_heads(q, k, v):
    scale = XA_HEAD_DIM ** -0.5
    outs = []
    for h in range(XA_HEADS):
        sl = slice(h * XA_HEAD_DIM, (h + 1) * XA_HEAD_DIM)
        s = lax.dot_general(q[:, sl], k[:, sl], _NT, preferred_element_type=F32) * scale
        e = jnp.exp(s - jnp.max(s, axis=-1, keepdims=True))
        p = e / jnp.sum(e, axis=-1, keepdims=True)
        outs.append(jnp.dot(p.astype(BF), v[:, sl], preferred_element_type=F32))
    return jnp.concatenate(outs, axis=-1)


def _attn_prompt_kernel(q_ref, k_ref, v_ref, o_ref):
    o_ref[...] = _attn_heads(q_ref[...], k_ref[...].astype(BF), v_ref[...].astype(BF)).astype(o_ref.dtype)


def _attn_prompt(q3, k3, v3):
    bn, tn, _ = q3.shape
    tm = min(ROW_TILE, tn)
    kv = pl.BlockSpec((None, N_MEM, WIDTH), lambda bi, i: (bi, 0, 0))
    qs = pl.BlockSpec((None, tm, WIDTH), lambda bi, i: (bi, i, 0))
    return pl.pallas_call(
        _attn_prompt_kernel,
        grid=(bn, tn // tm),
        in_specs=[qs, kv, kv],
        out_specs=qs,
        out_shape=jax.ShapeDtypeStruct((bn, tn, WIDTH), BF),
        compiler_params=_params(("parallel", "parallel")),
        name="xattn_prompt",
    )(q3, k3, v3)


def _attn_sample_kernel(bb, tq, q_ref, k_ref, v_ref, o_ref):
    nr = XA_HEADS * tq
    nc = N_MEM * XA_HEADS
    row_head = lax.broadcasted_iota(jnp.int32, (nr, nc), 0) // tq
    col_head = lax.broadcasted_iota(jnp.int32, (nr, nc), 1) % XA_HEADS
    own = row_head == col_head
    scale = XA_HEAD_DIM ** -0.5
    for b in range(bb):
        qb = q_ref[b]
        qrows = jnp.concatenate([qb[:, h * XA_HEAD_DIM:(h + 1) * XA_HEAD_DIM] for h in range(XA_HEADS)],
                                axis=0).astype(BF)
        s = lax.dot_general(qrows, k_ref[b].astype(BF), _NT, preferred_element_type=F32) * scale
        s = jnp.where(own, s, -jnp.inf)
        e = jnp.exp(s - jnp.max(s, axis=-1, keepdims=True))
        p = e / jnp.sum(e, axis=-1, keepdims=True)
        of = jnp.dot(p.astype(BF), v_ref[b].astype(BF), preferred_element_type=F32)
        o_ref[b] = jnp.concatenate([of[h * tq:(h + 1) * tq] for h in range(XA_HEADS)], axis=1)


def _attn_sample(q3, k3, v3):
    bn, tq, _ = q3.shape
    bb = 8
    kv = pl.BlockSpec((bb, N_MEM * XA_HEADS, XA_HEAD_DIM), lambda i: (i, 0, 0))
    qs = pl.BlockSpec((bb, tq, WIDTH), lambda i: (i, 0, 0))
    return pl.pallas_call(
        functools.partial(_attn_sample_kernel, bb, tq),
        grid=(bn // bb,),
        in_specs=[qs, kv, kv],
        out_specs=qs,
        out_shape=jax.ShapeDtypeStruct((bn, tq, WIDTH), F32),
        compiler_params=_params(("parallel",)),
        name="xattn_sample",
    )(q3, k3, v3)


def _merge_kernel(x_ref, s5_ref, cv_ref, xa_ref, g_ref, wg_ref, bg_ref, ws5_ref, wcv_ref, wxa_ref, wout_ref, h_ref):
    x = x_ref[...]
    n = _rms(x, g_ref[...]).astype(BF)
    gate = _sigmoid(jnp.dot(n, wg_ref[...], preferred_element_type=F32) + bg_ref[...])
    merged = (gate[:, :D_MODEL] * jnp.dot(s5_ref[...], ws5_ref[...], preferred_element_type=F32)
              + gate[:, D_MODEL:2 * D_MODEL] * jnp.dot(cv_ref[...], wcv_ref[...], preferred_element_type=F32)
              + gate[:, 2 * D_MODEL:] * jnp.dot(xa_ref[...], wxa_ref[...], preferred_element_type=F32))
    h_ref[...] = x + jnp.dot(merged.astype(BF), wout_ref[...], preferred_element_type=F32)


def _merge(x2, s5, cv, xa, g, wg, bg, ws5, wcv, wxa, wout):
    rows = x2.shape[0]
    xs = pl.BlockSpec((ROW_TILE, D_MODEL), lambda i: (i, 0))
    br = pl.BlockSpec((ROW_TILE, WIDTH), lambda i: (i, 0))
    proj = _const((WIDTH, D_MODEL))
    return pl.pallas_call(
        _merge_kernel,
        grid=(rows // ROW_TILE,),
        in_specs=[xs, br, br, br, _const((1, D_MODEL)), _const((D_MODEL, 3 * D_MODEL)), _const((1, 3 * D_MODEL)),
                  proj, proj, proj, _const((D_MODEL, D_MODEL))],
        out_specs=xs,
        out_shape=jax.ShapeDtypeStruct((rows, D_MODEL), F32),
        compiler_params=_params(("parallel",)),
        name="merge_out",
    )(x2, s5, cv, xa, g, wg, bg, ws5, wcv, wxa, wout)


def _top_values(work, count):
    vals = []
    for _ in range(count):
        m = jnp.max(work, axis=0, keepdims=True)
        vals.append(m)
        work = jnp.where(work == m, -jnp.inf, work)
    return vals


def _sorting_network(n):
    pairs = []

    def merge(lo, cnt, r):
        step = 2 * r
        if step < cnt:
            merge(lo, cnt, step)
            merge(lo + r, cnt, step)
            pairs.extend((k, k + r) for k in range(lo + r, lo + cnt - r, step))
        else:
            pairs.append((lo, lo + r))

    def sort(lo, cnt):
        if cnt > 1:
            sort(lo, cnt // 2)
            sort(lo + cnt // 2, cnt // 2)
            merge(lo, cnt, 1)

    sort(0, n)
    return pairs


def _top_values_sorted(work, count):
    cols = [work[8 * k:8 * (k + 1)] for k in range(work.shape[0] // 8)]
    for lo, hi in _sorting_network(len(cols)):
        cols[lo], cols[hi] = jnp.maximum(cols[lo], cols[hi]), jnp.minimum(cols[lo], cols[hi])
    vals = []
    for k in range(count):
        m = jnp.max(cols[0], axis=0, keepdims=True)
        vals.append(m)
        needed = count - 1 - k
        if needed == 0:
            break
        hit = cols[0] == m
        cols = [jnp.where(hit, cols[d + 1] if d + 1 < len(cols) else -jnp.inf, cols[d])
                for d in range(min(len(cols), needed))]
    return vals


def _top_pair_sums(sv1, sv2, row8, count):
    assert 8 < count < 18
    main = []
    for r in range(count):
        limit = count // (r + 1)
        v = sv1[r:r + 1] + sv2[0:8]
        main.append(v if limit >= 8 else jnp.where(row8 < limit, v, -jnp.inf))
    groups = [main, [sv1[0:1] + sv2[8:16]], [sv1[0:1] + sv2[16:24]]]
    vals = []
    for k in range(count):
        head = groups[0][0]
        for g in groups[1:]:
            head = jnp.maximum(head, g[0])
        m = jnp.max(head, axis=0, keepdims=True)
        vals.append(m)
        needed = count - 1 - k
        if needed == 0:
            break
        shifted = []
        for g in groups:
            hit = g[0] == m
            shifted.append([jnp.where(hit, g[d + 1] if d + 1 < len(g) else -jnp.inf, g[d])
                            for d in range(min(len(g), needed))])
        groups = shifted
    return vals


def _peer_kernel(n_esteps, h_ref, g2_ref, wqt_ref, keys_ref, u_ref, vt_ref, gf_ref, y_ref,
                 x2_scr, qt_scr, st_scr, sv_scr, e1_scr, e2_scr, cz_scr, at_scr, mt_scr, acc_scr):
    j = pl.program_id(1)
    tn = PEER_TOKENS
    nk = PEER_TOPK + 1
    lanes = 128

    @pl.when(j == 0)
    def _():
        x2 = _rms(h_ref[...], g2_ref[...]).astype(BF)
        x2_scr[...] = x2
        qt_scr[...] = lax.dot_general(wqt_ref[...], x2, _NT, preferred_element_type=F32)
        for hp in range(2 * PEER_HEADS):
            st_scr[hp] = jnp.dot(keys_ref[hp % 2], qt_scr[hp * PEER_NKEYS:(hp + 1) * PEER_NKEYS, :].astype(BF),
                                 preferred_element_type=F32)

        def half_body(hp, carry):
            for lt in range(tn // lanes):
                ls = slice(lt * lanes, (lt + 1) * lanes)
                vals = _top_values_sorted(st_scr[hp, :, ls], nk)
                for k in range(nk):
                    sv_scr[hp, k:k + 1, ls] = vals[k]
                sv_scr[hp, nk:24, ls] = jnp.full((24 - nk, lanes), -jnp.inf, F32)
            return carry

        lax.fori_loop(0, 2 * PEER_HEADS, half_body, 0)

        row8 = lax.broadcasted_iota(jnp.int32, (8, lanes), 0)

        def head_body(h, carry):
            for lt in range(tn // lanes):
                ls = slice(lt * lanes, (lt + 1) * lanes)
                sv1 = sv_scr[2 * h, :, ls]
                sv2 = sv_scr[2 * h + 1, :, ls]
                cv = _top_pair_sums(sv1, sv2, row8, nk)
                z = jnp.ones_like(cv[0])
                for k in range(1, PEER_TOPK):
                    z = z + jnp.exp(cv[k] - cv[0])
                tau = 0.5 * (cv[PEER_TOPK - 1] + cv[PEER_TOPK])
                hz = 0.5 / z
                cz_scr[h, :, ls] = jnp.broadcast_to(jnp.exp(tau - cv[0]) * hz, (8, lanes))
                e1_scr[h, :, ls] = jnp.exp(st_scr[2 * h, :, ls] - sv1[0:1]) * hz
                e2_scr[h, :, ls] = jnp.exp(st_scr[2 * h + 1, :, ls] - sv2[0:1])
            return carry

        lax.fori_loop(0, PEER_HEADS, head_body, 0)
        acc_scr[...] = jnp.zeros_like(acc_scr)

    ch = PEER_CHUNK
    n_chunks = PEER_EXPERTS // ch

    def act_piece(c, nh):
        ns = slice(nh * PEER_ACT_LANES, (nh + 1) * PEER_ACT_LANES)
        at_scr[c % 2, :, ns] = lax.dot_general(u_ref[c * ch:(c + 1) * ch, :], x2_scr[ns, :], _NT,
                                               preferred_element_type=F32)

    def value_piece(c, mq):
        ms = slice(mq * PEER_VROWS, (mq + 1) * PEER_VROWS)
        acc_scr[ms, :] += jnp.dot(vt_ref[ms, c * ch:(c + 1) * ch], mt_scr[c % 2], preferred_element_type=F32)

    def gate_blocks(c):
        blocks = []
        for a in range(ch // PEER_NKEYS):
            for lw in range(tn // PEER_LANES):
                shared = {}

                def block(rb, a=a, lw=lw, shared=shared):
                    ls = slice(lw * PEER_LANES, (lw + 1) * PEER_LANES)
                    if not shared:
                        i1 = j * (PEER_EXPERTS // PEER_NKEYS) + (c * ch) // PEER_NKEYS + a
                        shared["e1"] = [jnp.broadcast_to(e1_scr[h, pl.ds(i1, 1), ls], (PEER_ROWS, PEER_LANES))
                                        for h in range(PEER_HEADS)]
                        shared["cz"] = [jnp.concatenate([cz_scr[h, :, ls]] * (PEER_ROWS // 8), axis=0)
                                        for h in range(PEER_HEADS)]
                    i2 = rb * PEER_ROWS
                    r0 = a * PEER_NKEYS + i2
                    xa = at_scr[c % 2, r0:r0 + PEER_ROWS, ls]
                    act = xa + xa * lax.erf(xa * (1.0 / math.sqrt(2.0)))
                    w = jnp.zeros((PEER_ROWS, PEER_LANES), F32)
                    for h in range(PEER_HEADS):
                        p = shared["e1"][h] * e2_scr[h, i2:i2 + PEER_ROWS, ls]
                        w = w + jnp.where(p >= shared["cz"][h], p, 0.0)
                    mt_scr[c % 2, r0:r0 + PEER_ROWS, ls] = (w * act).astype(BF)

                blocks += [functools.partial(block, rb) for rb in range(PEER_NKEYS // PEER_ROWS)]
        return blocks

    for nh in range(tn // PEER_ACT_LANES):
        act_piece(0, nh)
    for c in range(n_chunks):
        pieces = []
        if c + 1 < n_chunks:
            pieces += [functools.partial(act_piece, c + 1, nh) for nh in range(tn // PEER_ACT_LANES)]
        if c >= 1:
            pieces += [functools.partial(value_piece, c - 1, mq) for mq in range(D_MODEL // PEER_VROWS)]
        blocks = gate_blocks(c)
        for bi, blk in enumerate(blocks):
            for pi, piece in enumerate(pieces):
                if (pi * len(blocks)) // len(pieces) == bi:
                    piece()
            blk()
    for mq in range(D_MODEL // PEER_VROWS):
        value_piece(n_chunks - 1, mq)

    @pl.when(j == n_esteps - 1)
    def _():
        h2 = h_ref[...] + acc_scr[...].T
        y_ref[...] = _rms(h2, gf_ref[...])


def _peer_final(h2d, g2, wqt, keys, u_bf, vt_bf, gf):
    rows = h2d.shape[0]
    tn = PEER_TOKENS
    n_esteps = PEER_N_EXPERTS // PEER_EXPERTS
    hs = pl.BlockSpec((tn, D_MODEL), lambda i, j: (i, 0))
    return pl.pallas_call(
        functools.partial(_peer_kernel, n_esteps),
        grid=(rows // tn, n_esteps),
        in_specs=[hs, _const((1, D_MODEL)), _const((2 * PEER_HEADS * PEER_NKEYS, D_MODEL)),
                  _const((2, PEER_NKEYS, PEER_NKEYS)),
                  pl.BlockSpec((PEER_EXPERTS, D_MODEL), lambda i, j: (j, 0)),
                  pl.BlockSpec((D_MODEL, PEER_EXPERTS), lambda i, j: (0, j)), _const((1, D_MODEL))],
        out_specs=hs,
        out_shape=jax.ShapeDtypeStruct((rows, D_MODEL), F32),
        scratch_shapes=[pltpu.VMEM((tn, D_MODEL), BF),
                        pltpu.VMEM((2 * PEER_HEADS * PEER_NKEYS, tn), F32),
                        pltpu.VMEM((2 * PEER_HEADS, PEER_NKEYS, tn), F32),
                        pltpu.VMEM((2 * PEER_HEADS, 24, tn), F32),
                        pltpu.VMEM((PEER_HEADS, PEER_NKEYS, tn), F32),
                        pltpu.VMEM((PEER_HEADS, PEER_NKEYS, tn), F32),
                        pltpu.VMEM((PEER_HEADS, 8, tn), F32),
                        pltpu.VMEM((2, PEER_CHUNK, tn), F32),
                        pltpu.VMEM((2, PEER_CHUNK, tn), BF),
                        pltpu.VMEM((D_MODEL, tn), F32)],
        compiler_params=_params(("parallel", "arbitrary")),
        name="peer_final",
    )(h2d, g2, wqt, keys, u_bf, vt_bf, gf)


def _block_diag(w):
    g, r, c = w.shape
    idx = jnp.arange(g)
    return jnp.zeros((g, r, g, c), w.dtype).at[idx, :, idx, :].set(w).reshape(g * r, g * c)


def _row(v):
    return v.reshape(1, -1).astype(F32)


def kernel(x_prompt, x_sample, mem_prompt, state_ssm_re, state_ssm_im, state_conv, cache_mem_k, cache_mem_v,
           norm1_g, w_in, b_in, s5_log_dt, s5_a_re, s5_a_im, s5_b_re, s5_b_im, s5_c_re, s5_c_im, s5_d,
           s5_glu_w, s5_glu_b, w_s5_proj, conv_dw_w, conv_dw_b, conv_ln_g, conv_ln_b, w_conv_proj, mem_norm_g,
           w_mem_k, w_mem_v, w_xa_proj, w_out, norm2_g, peer_w_q, peer_sub_keys, peer_u, peer_v, final_norm_g):
    assert norm1_g.shape[0] == 1, "single-layer trunk"
    bp, tp, _ = x_prompt.shape
    bs, ts, _ = x_sample.shape
    n_s = bs * ts
    gps = S5_GROUPS // S5_SUPER

    w_in2 = w_in.reshape(D_MODEL, -1)
    b_in1 = b_in.reshape(-1)
    w1 = w_in2[:, :4 * WIDTH].astype(BF)
    b1 = _row(b_in1[:4 * WIDTH])
    wg = w_in2[:, 4 * WIDTH:].astype(BF)
    bg = _row(b_in1[4 * WIDTH:])

    def per_super(w):
        return jnp.stack([_block_diag(w[s * gps:(s + 1) * gps]) for s in range(S5_SUPER)])

    b_re_t = jnp.swapaxes(s5_b_re.reshape(S5_GROUPS, S5_STATE, S5_GROUP), 1, 2)
    b_im_t = jnp.swapaxes(s5_b_im.reshape(S5_GROUPS, S5_STATE, S5_GROUP), 1, 2)
    c_re_t = jnp.swapaxes(s5_c_re.reshape(S5_GROUPS, S5_GROUP, S5_STATE), 1, 2)
    c_im_t = jnp.swapaxes(s5_c_im.reshape(S5_GROUPS, S5_GROUP, S5_STATE), 1, 2)
    sw = {
        "bmat": jnp.concatenate([per_super(b_re_t), per_super(b_im_t)], axis=2).astype(BF),
        "cmat": jnp.concatenate([per_super(c_re_t), -per_super(c_im_t)], axis=1).astype(BF),
        "dskip": _row(s5_d),
        "ldt": _row(jnp.repeat(s5_log_dt.reshape(-1), S5_STATE)),
        "are": _row(s5_a_re),
        "aim": _row(s5_a_im),
        "dww": conv_dw_w.reshape(CONV_K, WIDTH).astype(F32),
        "dwb": _row(conv_dw_b),
        "lng": _row(conv_ln_g),
        "lnb": _row(conv_ln_b),
        "gluw": s5_glu_w.reshape(WIDTH, WIDTH).astype(BF),
        "glub": _row(s5_glu_b),
    }
    g1 = _row(norm1_g)
    merge_w = (g1, wg, bg, w_s5_proj.reshape(WIDTH, D_MODEL).astype(BF), w_conv_proj.reshape(WIDTH, D_MODEL).astype(BF),
               w_xa_proj.reshape(WIDTH, D_MODEL).astype(BF), w_out.reshape(D_MODEL, D_MODEL).astype(BF))
    peer_w = (_row(norm2_g), peer_w_q.reshape(D_MODEL, -1).T.astype(BF),
              peer_sub_keys.reshape(2, PEER_NKEYS, PEER_NKEYS).astype(BF),
              peer_u.reshape(PEER_N_EXPERTS, D_MODEL).astype(BF),
              peer_v.reshape(PEER_N_EXPERTS, D_MODEL).T.astype(BF), _row(final_norm_g))

    mk, mv = _memkv(mem_prompt.reshape(bp * N_MEM, D_MODEL), _row(mem_norm_g),
                    w_mem_k.reshape(D_MODEL, WIDTH).astype(BF), w_mem_v.reshape(D_MODEL, WIDTH).astype(BF))
    tt_p = 64
    zeros_st = jnp.zeros((bp, S5_CH), F32)
    q_p, s5_p, cv_p, a_last, re_p, im_p = _seq(x_prompt, bp, tt_p, 512, zeros_st, zeros_st,
                                               jnp.zeros((HIST * bp, WIDTH), F32), g1, w1, b1, sw)
    xa_p = _attn_prompt(q_p, mk.reshape(bp, N_MEM, WIDTH), mv.reshape(bp, N_MEM, WIDTH))
    h_p = _merge(x_prompt.reshape(bp * tp, D_MODEL), s5_p.reshape(bp * tp, WIDTH), cv_p.reshape(bp * tp, WIDTH),
                 xa_p.reshape(bp * tp, WIDTH), *merge_w)
    y_prompt = _peer_final(h_p, *peer_w).reshape(bp, tp, D_MODEL)
    conv_p = jnp.swapaxes(a_last[(tt_p - HIST) * bp:].reshape(HIST, bp, WIDTH), 0, 1)

    xs2 = x_sample.reshape(n_s, D_MODEL)
    st_conv = state_conv.reshape(bs, HIST, WIDTH)
    hist_s = jnp.swapaxes(st_conv, 0, 1).reshape(HIST * bs, WIDTH)
    q_s, s5_s, cv_s, a_s, re_s, im_s = _seq(xs2, bs, ts, 128, state_ssm_re.reshape(bs, S5_CH),
                                            state_ssm_im.reshape(bs, S5_CH), hist_s, g1, w1, b1, sw)
    xa_s = _attn_sample(q_s.reshape(bs, ts, WIDTH).astype(F32),
                        cache_mem_k.reshape(bs, N_MEM * XA_HEADS, XA_HEAD_DIM),
                        cache_mem_v.reshape(bs, N_MEM * XA_HEADS, XA_HEAD_DIM))
    h_s = _merge(xs2, s5_s, cv_s, xa_s.reshape(n_s, WIDTH).astype(BF), *merge_w)
    y_sample = _peer_final(h_s, *peer_w).reshape(bs, ts, D_MODEL)
    conv_s = jnp.concatenate([st_conv[:, ts:], jnp.swapaxes(a_s.reshape(ts, bs, WIDTH), 0, 1)], axis=1)

    st_shape_p = (1, bp, S5_GROUPS, S5_STATE)
    st_shape_s = (1, bs, S5_GROUPS, S5_STATE)
    return (y_prompt, y_sample, re_p.reshape(st_shape_p), im_p.reshape(st_shape_p), conv_p[None],
            mk.reshape(1, bp, N_MEM, XA_HEADS, XA_HEAD_DIM), mv.reshape(1, bp, N_MEM, XA_HEADS, XA_HEAD_DIM),
            re_s.reshape(st_shape_s), im_s.reshape(st_shape_s), conv_s[None])
```

```python
import functools
import math

import jax
import jax.numpy as jnp
from jax import lax
from jax.experimental import pallas as pl
from jax.experimental.pallas import tpu as pltpu

F32 = jnp.float32
BF = jnp.bfloat16

D_MODEL = 1024
WIDTH = 512
S5_GROUPS = 32
S5_GROUP = 16
S5_STATE = 64
S5_CH = S5_GROUPS * S5_STATE
S5_SUPER = 2
CONV_K = 31
HIST = CONV_K - 1
N_MEM = 256
XA_HEADS = 4
XA_HEAD_DIM = 128
PEER_HEADS = 8
PEER_NKEYS = 128
PEER_N_EXPERTS = PEER_NKEYS * PEER_NKEYS
PEER_TOPK = 16
RMS_EPS = 1e-6
LN_EPS = 1e-5

ROW_TILE = 512
PEER_TOKENS = 512
PEER_EXPERTS = 2048
PEER_CHUNK = 512
PEER_ROWS = 16
PEER_LANES = 256
PEER_ACT_LANES = 512
PEER_VROWS = 1024
VMEM_LIMIT = 56 * 1024 * 1024

_NT = (((1,), (1,)), ((), ()))


def _params(sem):
    return pltpu.CompilerParams(dimension_semantics=sem, vmem_limit_bytes=VMEM_LIMIT)


def _rms(x, g):
    return x * lax.rsqrt(jnp.mean(x * x, axis=-1, keepdims=True) + RMS_EPS) * g


def _sigmoid(x):
    return 1.0 / (1.0 + jnp.exp(-x))


def _gelu(x):
    return 0.5 * x * (1.0 + lax.erf(x * (1.0 / math.sqrt(2.0))))


def _const(shape):
    nd = len(shape)
    return pl.BlockSpec(shape, lambda *_: (0,) * nd, pipeline_mode=pl.Buffered(1))


def _memkv_kernel(m_ref, g_ref, wk_ref, wv_ref, k_ref, v_ref):
    m = _rms(m_ref[...], g_ref[...]).astype(BF)
    k_ref[...] = jnp.dot(m, wk_ref[...], preferred_element_type=F32)
    v_ref[...] = jnp.dot(m, wv_ref[...], preferred_element_type=F32)


def _memkv(mem2d, g, wk, wv):
    rows = mem2d.shape[0]
    return pl.pallas_call(
        _memkv_kernel,
        grid=(rows // ROW_TILE,),
        in_specs=[pl.BlockSpec((ROW_TILE, D_MODEL), lambda i: (i, 0)), _const((1, D_MODEL)),
                  _const((D_MODEL, WIDTH)), _const((D_MODEL, WIDTH))],
        out_specs=[pl.BlockSpec((ROW_TILE, WIDTH), lambda i: (i, 0))] * 2,
        out_shape=[jax.ShapeDtypeStruct((rows, WIDTH), F32)] * 2,
        compiler_params=_params(("parallel",)),
        name="mem_kv",
    )(mem2d, g, wk, wv)


def _seq_kernel(nb, tt, cw, n_steps,
                x_ref, g1_ref, w1_ref, b1_ref, p_ref, pt_ref, bmat_ref, cmat_ref, dsk_ref, ldt_ref, are_ref, aim_ref,
                h0r_ref, h0i_ref, hist_ref, dww_ref, dwb_ref, lng_ref, lnb_ref, gluw_ref, glub_ref,
                q_ref, s5_ref, co_ref, a_ref, sre_ref, sim_ref,
                bu_scr, st_scr, par_scr, cb_scr, y_scr, u_scr, bbar_scr):
    i = pl.program_id(0)
    rt = nb * tt
    half = S5_CH // S5_SUPER
    uw = WIDTH // S5_SUPER

    @pl.when(i == 0)
    def _():
        st_scr[0] = h0r_ref[...]
        st_scr[1] = h0i_ref[...]
        cb_scr[0:HIST * nb, :] = hist_ref[...]
        dt = jnp.exp(ldt_ref[...])
        are = are_ref[...]
        aim = aim_ref[...]
        mag = jnp.exp(are * dt)
        lr = mag * jnp.cos(aim * dt)
        li = mag * jnp.sin(aim * dt)
        den = are * are + aim * aim
        nr = lr - 1.0
        par_scr[0:1, :] = lr
        par_scr[1:2, :] = li
        cr = (nr * are + li * aim) / den
        ci = (li * are - nr * aim) / den
        for s in range(S5_SUPER):
            crs = cr[:, half * s:half * (s + 1)]
            cis = ci[:, half * s:half * (s + 1)]
            b_re = bmat_ref[s, :, 0:half].astype(F32)
            b_im = bmat_ref[s, :, half:2 * half].astype(F32)
            bbar_scr[s, :, 0:half] = (b_re * crs - b_im * cis).astype(BF)
            bbar_scr[s, :, half:2 * half] = (b_im * crs + b_re * cis).astype(BF)

    n_bm = _rms(x_ref[...].reshape(rt, D_MODEL), g1_ref[...]).astype(BF)
    n_tm = jnp.dot(p_ref[...], n_bm, preferred_element_type=F32).astype(BF)

    def in_proj(lo, hi, n=n_tm):
        return jnp.dot(n, w1_ref[:, lo:hi], preferred_element_type=F32) + b1_ref[:, lo:hi]

    za = in_proj(WIDTH, 3 * WIDTH)
    a = za[:, :WIDTH] * _sigmoid(za[:, WIDTH:])
    a_ref[...] = a
    cb_scr[HIST * nb:HIST * nb + rt, :] = a

    def proj_u():
        u_scr[...] = in_proj(0, WIDTH)

    def proj_q():
        q_ref[...] = in_proj(3 * WIDTH, 4 * WIDTH, n=n_bm).astype(BF).reshape(q_ref.shape)

    def proj_b(s):
        bu_scr[:, 2 * half * s:2 * half * (s + 1)] = jnp.dot(u_scr[:, uw * s:uw * (s + 1)].astype(BF), bbar_scr[s],
                                                             preferred_element_type=F32)

    pieces = [proj_u, proj_q] + [functools.partial(proj_b, s) for s in range(S5_SUPER)]
    rb = 32
    n_blocks = rt // rb
    for bi in range(n_blocks):
        for pi, piece in enumerate(pieces):
            if (pi * n_blocks) // len(pieces) == bi:
                piece()
        r0 = bi * rb
        acc = jnp.zeros((rb, WIDTH), F32)
        for k in range(CONV_K):
            acc = acc + dww_ref[k:k + 1, :] * cb_scr[r0 + k * nb:r0 + k * nb + rb, :]
        y_scr[r0:r0 + rb, :] = acc
    if n_steps > 1:
        cb_scr[0:HIST * nb, :] = cb_scr[rt:rt + HIST * nb, :]

    yv = y_scr[...] + dwb_ref[...]
    yc = yv - jnp.mean(yv, axis=-1, keepdims=True)
    var = jnp.mean(yc * yc, axis=-1, keepdims=True)
    ln = yc * lax.rsqrt(var + LN_EPS) * lng_ref[...] + lnb_ref[...]
    co_tm = (ln * _sigmoid(ln)).astype(BF)
    co_ref[...] = jnp.dot(pt_ref[...], co_tm, preferred_element_type=F32).astype(BF).reshape(co_ref.shape)

    for s in range(S5_SUPER):
        for c in range(half // cw):
            nat = slice(half * s + c * cw, half * s + (c + 1) * cw)
            re_sl = slice(2 * half * s + c * cw, 2 * half * s + (c + 1) * cw)
            im_sl = slice(2 * half * s + half + c * cw, 2 * half * s + half + (c + 1) * cw)
            lr_b = jnp.broadcast_to(par_scr[0:1, nat], (nb, cw))
            li_b = jnp.broadcast_to(par_scr[1:2, nat], (nb, cw))

            def step(t, carry, re_sl=re_sl, im_sl=im_sl, lr_b=lr_b, li_b=li_b):
                s_re, s_im = carry
                r0 = pl.multiple_of(t * nb, nb)
                n_re = lr_b * s_re - li_b * s_im + bu_scr[pl.ds(r0, nb), re_sl]
                n_im = lr_b * s_im + li_b * s_re + bu_scr[pl.ds(r0, nb), im_sl]
                bu_scr[pl.ds(r0, nb), re_sl] = n_re
                bu_scr[pl.ds(r0, nb), im_sl] = n_im
                return n_re, n_im

            s_re, s_im = lax.fori_loop(0, tt, step, (st_scr[0, :, nat], st_scr[1, :, nat]), unroll=min(tt, 8))
            st_scr[0, :, nat] = s_re
            st_scr[1, :, nat] = s_im

    y = jnp.concatenate(
        [jnp.dot(bu_scr[:, 2 * half * s:2 * half * (s + 1)].astype(BF), cmat_ref[s], preferred_element_type=F32)
         for s in range(S5_SUPER)], axis=-1) + dsk_ref[...] * u_scr[...]
    zz = _gelu(y)
    gate = _sigmoid(jnp.dot(zz.astype(BF), gluw_ref[...], preferred_element_type=F32) + glub_ref[...])
    s5_tm = (zz * gate).astype(BF)
    s5_ref[...] = jnp.dot(pt_ref[...], s5_tm, preferred_element_type=F32).astype(BF).reshape(s5_ref.shape)

    @pl.when(i == n_steps - 1)
    def _():
        sre_ref[...] = st_scr[0]
        sim_ref[...] = st_scr[1]


def _seq(x, nb, tt, cw, h0r, h0i, hist, g1, w1, b1, sw):
    rt = nb * tt
    if x.ndim == 3:
        n_steps = x.shape[1] // tt
        xspec = pl.BlockSpec((nb, tt, D_MODEL), lambda i: (0, i, 0))
        ospec = pl.BlockSpec((nb, tt, WIDTH), lambda i: (0, i, 0))
        oshape = jax.ShapeDtypeStruct((nb, x.shape[1], WIDTH), BF)
    else:
        n_steps = 1
        xspec = pl.BlockSpec((rt, D_MODEL), lambda i: (0, 0))
        ospec = pl.BlockSpec((rt, WIDTH), lambda i: (0, 0))
        oshape = jax.ShapeDtypeStruct((rt, WIDTH), BF)
    idx = jnp.arange(rt)
    tm_of_bm = (idx % tt) * nb + idx // tt
    perm = (idx[:, None] == tm_of_bm[None, :]).astype(BF)
    perm_t = (tm_of_bm[:, None] == idx[None, :]).astype(BF)
    st = _const((nb, S5_CH))
    vec = _const((1, WIDTH))
    chv = _const((1, S5_CH))
    half = S5_CH // S5_SUPER
    return pl.pallas_call(
        functools.partial(_seq_kernel, nb, tt, cw, n_steps),
        grid=(n_steps,),
        in_specs=[xspec, _const((1, D_MODEL)), _const(w1.shape), _const(b1.shape),
                  _const((rt, rt)), _const((rt, rt)),
                  _const((S5_SUPER, WIDTH // S5_SUPER, 2 * half)), _const((S5_SUPER, 2 * half, WIDTH // S5_SUPER)),
                  vec, chv, chv, chv, st, st, _const((HIST * nb, WIDTH)), _const((CONV_K, WIDTH)), vec, vec, vec,
                  _const((WIDTH, WIDTH)), vec],
        out_specs=[ospec, ospec, ospec, _const((rt, WIDTH)), st, st],
        out_shape=[oshape, oshape, oshape, jax.ShapeDtypeStruct((rt, WIDTH), F32),
                   jax.ShapeDtypeStruct((nb, S5_CH), F32), jax.ShapeDtypeStruct((nb, S5_CH), F32)],
        scratch_shapes=[pltpu.VMEM((rt, 2 * S5_CH), F32), pltpu.VMEM((2, nb, S5_CH), F32),
                        pltpu.VMEM((8, S5_CH), F32), pltpu.VMEM(((HIST + tt) * nb, WIDTH), F32),
                        pltpu.VMEM((rt, WIDTH), F32), pltpu.VMEM((rt, WIDTH), F32),
                        pltpu.VMEM((S5_SUPER, WIDTH // S5_SUPER, 2 * half), BF)],
        compiler_params=_params(("arbitrary",)),
        name="s5_conv",
    )(x, g1, w1, b1, perm, perm_t, sw["bmat"], sw["cmat"], sw["dskip"], sw["ldt"], sw["are"], sw["aim"],
      h0r, h0i, hist, sw["dww"], sw["dwb"], sw["lng"], sw["lnb"], sw["gluw"], sw["glub"])


def _attn_heads(q, k, v):
    scale = XA_HEAD_DIM ** -0.5
    outs = []
    for h in range(XA_HEADS):
        sl = slice(h * XA_HEAD_DIM, (h + 1) * XA_HEAD_DIM)
        s = lax.dot_general(q[:, sl], k[:, sl], _NT, preferred_element_type=F32) * scale
        e = jnp.exp(s - jnp.max(s, axis=-1, keepdims=True))
        p = e / jnp.sum(e, axis=-1, keepdims=True)
        outs.append(jnp.dot(p.astype(BF), v[:, sl], preferred_element_type=F32))
    return jnp.concatenate(outs, axis=-1)


def _attn_prompt_kernel(q_ref, k_ref, v_ref, o_ref):
    o_ref[...] = _attn_heads(q_ref[...], k_ref[...].astype(BF), v_ref[...].astype(BF)).astype(o_ref.dtype)


def _attn_prompt(q3, k3, v3):
    bn, tn, _ = q3.shape
    tm = min(ROW_TILE, tn)
    kv = pl.BlockSpec((None, N_MEM, WIDTH), lambda bi, i: (bi, 0, 0))
    qs = pl.BlockSpec((None, tm, WIDTH), lambda bi, i: (bi, i, 0))
    return pl.pallas_call(
        _attn_prompt_kernel,
        grid=(bn, tn // tm),
        in_specs=[qs, kv, kv],
        out_specs=qs,
        out_shape=jax.ShapeDtypeStruct((bn, tn, WIDTH), BF),
        compiler_params=_params(("parallel", "parallel")),
        name="xattn_prompt",
    )(q3, k3, v3)


def _attn_sample_kernel(bb, tq, q_ref, k_ref, v_ref, o_ref):
    nr = XA_HEADS * tq
    nc = N_MEM * XA_HEADS
    row_head = lax.broadcasted_iota(jnp.int32, (nr, nc), 0) // tq
    col_head = lax.broadcasted_iota(jnp.int32, (nr, nc), 1) % XA_HEADS
    own = row_head == col_head
    scale = XA_HEAD_DIM ** -0.5
    for b in range(bb):
        qb = q_ref[b]
        qrows = jnp.concatenate([qb[:, h * XA_HEAD_DIM:(h + 1) * XA_HEAD_DIM] for h in range(XA_HEADS)],
                                axis=0).astype(BF)
        s = lax.dot_general(qrows, k_ref[b].astype(BF), _NT, preferred_element_type=F32) * scale
        s = jnp.where(own, s, -jnp.inf)
        e = jnp.exp(s - jnp.max(s, axis=-1, keepdims=True))
        p = e / jnp.sum(e, axis=-1, keepdims=True)
        of = jnp.dot(p.astype(BF), v_ref[b].astype(BF), preferred_element_type=F32)
        o_ref[b] = jnp.concatenate([of[h * tq:(h + 1) * tq] for h in range(XA_HEADS)], axis=1)


def _attn_sample(q3, k3, v3):
    bn, tq, _ = q3.shape
    bb = 8
    kv = pl.BlockSpec((bb, N_MEM * XA_HEADS, XA_HEAD_DIM), lambda i: (i, 0, 0))
    qs = pl.BlockSpec((bb, tq, WIDTH), lambda i: (i, 0, 0))
    return pl.pallas_call(
        functools.partial(_attn_sample_kernel, bb, tq),
        grid=(bn // bb,),
        in_specs=[qs, kv, kv],
        out_specs=qs,
        out_shape=jax.ShapeDtypeStruct((bn, tq, WIDTH), F32),
        compiler_params=_params(("parallel",)),
        name="xattn_sample",
    )(q3, k3, v3)


def _merge_kernel(x_ref, s5_ref, cv_ref, xa_ref, g_ref, wg_ref, bg_ref, ws5_ref, wcv_ref, wxa_ref, wout_ref, h_ref):
    x = x_ref[...]
    n = _rms(x, g_ref[...]).astype(BF)
    gate = _sigmoid(jnp.dot(n, wg_ref[:, 4 * WIDTH:], preferred_element_type=F32) + bg_ref[:, 4 * WIDTH:])
    merged = (gate[:, :D_MODEL] * jnp.dot(s5_ref[...], ws5_ref[...], preferred_element_type=F32)
              + gate[:, D_MODEL:2 * D_MODEL] * jnp.dot(cv_ref[...], wcv_ref[...], preferred_element_type=F32)
              + gate[:, 2 * D_MODEL:] * jnp.dot(xa_ref[...], wxa_ref[...], preferred_element_type=F32))
    h_ref[...] = x + jnp.dot(merged.astype(BF), wout_ref[...], preferred_element_type=F32)


def _merge(x2, s5, cv, xa, g, wg, bg, ws5, wcv, wxa, wout):
    rows = x2.shape[0]
    xs = pl.BlockSpec((ROW_TILE, D_MODEL), lambda i: (i, 0))
    br = pl.BlockSpec((ROW_TILE, WIDTH), lambda i: (i, 0))
    proj = _const((WIDTH, D_MODEL))
    return pl.pallas_call(
        _merge_kernel,
        grid=(rows // ROW_TILE,),
        in_specs=[xs, br, br, br, _const((1, D_MODEL)), _const(wg.shape), _const(bg.shape),
                  proj, proj, proj, _const((D_MODEL, D_MODEL))],
        out_specs=xs,
        out_shape=jax.ShapeDtypeStruct((rows, D_MODEL), F32),
        compiler_params=_params(("parallel",)),
        name="merge_out",
    )(x2, s5, cv, xa, g, wg, bg, ws5, wcv, wxa, wout)


def _top_values(work, count):
    vals = []
    for _ in range(count):
        m = jnp.max(work, axis=0, keepdims=True)
        vals.append(m)
        work = jnp.where(work == m, -jnp.inf, work)
    return vals


def _sorting_network(n):
    pairs = []

    def merge(lo, cnt, r):
        step = 2 * r
        if step < cnt:
            merge(lo, cnt, step)
            merge(lo + r, cnt, step)
            pairs.extend((k, k + r) for k in range(lo + r, lo + cnt - r, step))
        else:
            pairs.append((lo, lo + r))

    def sort(lo, cnt):
        if cnt > 1:
            sort(lo, cnt // 2)
            sort(lo + cnt // 2, cnt // 2)
            merge(lo, cnt, 1)

    sort(0, n)
    return pairs


def _top_values_sorted(work, count):
    cols = [work[8 * k:8 * (k + 1)] for k in range(work.shape[0] // 8)]
    for lo, hi in _sorting_network(len(cols)):
        cols[lo], cols[hi] = jnp.maximum(cols[lo], cols[hi]), jnp.minimum(cols[lo], cols[hi])
    vals = []
    for k in range(count):
        m = jnp.max(cols[0], axis=0, keepdims=True)
        vals.append(m)
        needed = count - 1 - k
        if needed == 0:
            break
        hit = cols[0] == m
        cols = [jnp.where(hit, cols[d + 1] if d + 1 < len(cols) else -jnp.inf, cols[d])
                for d in range(min(len(cols), needed))]
    return vals


def _top_pair_sums(sv1, sv2, row8, count):
    assert 8 < count < 18
    main = []
    for r in range(count):
        limit = count // (r + 1)
        v = sv1[r:r + 1] + sv2[0:8]
        main.append(v if limit >= 8 else jnp.where(row8 < limit, v, -jnp.inf))
    groups = [main, [sv1[0:1] + sv2[8:16]], [sv1[0:1] + sv2[16:24]]]
    vals = []
    for k in range(count):
        head = groups[0][0]
        for g in groups[1:]:
            head = jnp.maximum(head, g[0])
        m = jnp.max(head, axis=0, keepdims=True)
        vals.append(m)
        needed = count - 1 - k
        if needed == 0:
            break
        shifted = []
        for g in groups:
            hit = g[0] == m
            shifted.append([jnp.where(hit, g[d + 1] if d + 1 < len(g) else -jnp.inf, g[d])
                            for d in range(min(len(g), needed))])
        groups = shifted
    return vals


def _peer_kernel(n_esteps, h_ref, g2_ref, wqt_ref, keys_ref, u_ref, vt_ref, gf_ref, y_ref,
                 x2_scr, qt_scr, st_scr, sv_scr, e1_scr, e2_scr, cz_scr, at_scr, mt_scr, acc_scr):
    j = pl.program_id(1)
    tn = PEER_TOKENS
    nk = PEER_TOPK + 1
    lanes = 128

    @pl.when(j == 0)
    def _():
        x2 = _rms(h_ref[...], g2_ref[...]).astype(BF)
        x2_scr[...] = x2
        qt_scr[...] = lax.dot_general(wqt_ref[...], x2, _NT, preferred_element_type=F32)
        for hp in range(2 * PEER_HEADS):
            st_scr[hp] = jnp.dot(keys_ref[hp % 2], qt_scr[hp * PEER_NKEYS:(hp + 1) * PEER_NKEYS, :].astype(BF),
                                 preferred_element_type=F32)

        def half_body(hp, carry):
            for lt in range(tn // lanes):
                ls = slice(lt * lanes, (lt + 1) * lanes)
                vals = _top_values_sorted(st_scr[hp, :, ls], nk)
                for k in range(nk):
                    sv_scr[hp, k:k + 1, ls] = vals[k]
                sv_scr[hp, nk:24, ls] = jnp.full((24 - nk, lanes), -jnp.inf, F32)
            return carry

        lax.fori_loop(0, 2 * PEER_HEADS, half_body, 0)

        row8 = lax.broadcasted_iota(jnp.int32, (8, lanes), 0)

        def head_body(h, carry):
            for lt in range(tn // lanes):
                ls = slice(lt * lanes, (lt + 1) * lanes)
                sv1 = sv_scr[2 * h, :, ls]
                sv2 = sv_scr[2 * h + 1, :, ls]
                cv = _top_pair_sums(sv1, sv2, row8, nk)
                z = jnp.ones_like(cv[0])
                for k in range(1, PEER_TOPK):
                    z = z + jnp.exp(cv[k] - cv[0])
                tau = 0.5 * (cv[PEER_TOPK - 1] + cv[PEER_TOPK])
                hz = 0.5 / z
                cz_scr[h, :, ls] = jnp.broadcast_to(jnp.exp(tau - cv[0]) * hz, (8, lanes))
                e1_scr[h, :, ls] = jnp.exp(st_scr[2 * h, :, ls] - sv1[0:1]) * hz
                e2_scr[h, :, ls] = jnp.exp(st_scr[2 * h + 1, :, ls] - sv2[0:1])
            return carry

        lax.fori_loop(0, PEER_HEADS, head_body, 0)
        acc_scr[...] = jnp.zeros_like(acc_scr)

    ch = PEER_CHUNK
    n_chunks = PEER_EXPERTS // ch

    def act_piece(c, nh):
        ns = slice(nh * PEER_ACT_LANES, (nh + 1) * PEER_ACT_LANES)
        at_scr[c % 2, :, ns] = lax.dot_general(u_ref[c * ch:(c + 1) * ch, :], x2_scr[ns, :], _NT,
                                               preferred_element_type=F32)

    def value_piece(c, mq):
        ms = slice(mq * PEER_VROWS, (mq + 1) * PEER_VROWS)
        acc_scr[ms, :] += jnp.dot(vt_ref[ms, c * ch:(c + 1) * ch], mt_scr[c % 2], preferred_element_type=F32)

    def gate_blocks(c):
        blocks = []
        for a in range(ch // PEER_NKEYS):
            for lw in range(tn // PEER_LANES):
                shared = {}

                def block(rb, a=a, lw=lw, shared=shared):
                    ls = slice(lw * PEER_LANES, (lw + 1) * PEER_LANES)
                    if not shared:
                        i1 = j * (PEER_EXPERTS // PEER_NKEYS) + (c * ch) // PEER_NKEYS + a
                        shared["e1"] = [jnp.broadcast_to(e1_scr[h, pl.ds(i1, 1), ls], (PEER_ROWS, PEER_LANES))
                                        for h in range(PEER_HEADS)]
                        shared["cz"] = [jnp.concatenate([cz_scr[h, :, ls]] * (PEER_ROWS // 8), axis=0)
                                        for h in range(PEER_HEADS)]
                    i2 = rb * PEER_ROWS
                    r0 = a * PEER_NKEYS + i2
                    xa = at_scr[c % 2, r0:r0 + PEER_ROWS, ls]
                    act = xa + xa * lax.erf(xa * (1.0 / math.sqrt(2.0)))
                    w = jnp.zeros((PEER_ROWS, PEER_LANES), F32)
                    for h in range(PEER_HEADS):
                        p = shared["e1"][h] * e2_scr[h, i2:i2 + PEER_ROWS, ls]
                        w = w + jnp.where(p >= shared["cz"][h], p, 0.0)
                    mt_scr[c % 2, r0:r0 + PEER_ROWS, ls] = (w * act).astype(BF)

                blocks += [functools.partial(block, rb) for rb in range(PEER_NKEYS // PEER_ROWS)]
        return blocks

    for nh in range(tn // PEER_ACT_LANES):
        act_piece(0, nh)
    for c in range(n_chunks):
        pieces = []
        if c + 1 < n_chunks:
            pieces += [functools.partial(act_piece, c + 1, nh) for nh in range(tn // PEER_ACT_LANES)]
        if c >= 1:
            pieces += [functools.partial(value_piece, c - 1, mq) for mq in range(D_MODEL // PEER_VROWS)]
        blocks = gate_blocks(c)
        for bi, blk in enumerate(blocks):
            for pi, piece in enumerate(pieces):
                if (pi * len(blocks)) // len(pieces) == bi:
                    piece()
            blk()
    for mq in range(D_MODEL // PEER_VROWS):
        value_piece(n_chunks - 1, mq)

    @pl.when(j == n_esteps - 1)
    def _():
        h2 = h_ref[...] + acc_scr[...].T
        y_ref[...] = _rms(h2, gf_ref[...])


def _peer_final(h2d, g2, wqt, keys, u_bf, vt_bf, gf):
    rows = h2d.shape[0]
    tn = PEER_TOKENS
    n_esteps = PEER_N_EXPERTS // PEER_EXPERTS
    hs = pl.BlockSpec((tn, D_MODEL), lambda i, j: (i, 0))
    return pl.pallas_call(
        functools.partial(_peer_kernel, n_esteps),
        grid=(rows // tn, n_esteps),
        in_specs=[hs, _const((1, D_MODEL)), _const((2 * PEER_HEADS * PEER_NKEYS, D_MODEL)),
                  _const((2, PEER_NKEYS, PEER_NKEYS)),
                  pl.BlockSpec((PEER_EXPERTS, D_MODEL), lambda i, j: (j, 0)),
                  pl.BlockSpec((D_MODEL, PEER_EXPERTS), lambda i, j: (0, j)), _const((1, D_MODEL))],
        out_specs=hs,
        out_shape=jax.ShapeDtypeStruct((rows, D_MODEL), F32),
        scratch_shapes=[pltpu.VMEM((tn, D_MODEL), BF),
                        pltpu.VMEM((2 * PEER_HEADS * PEER_NKEYS, tn), F32),
                        pltpu.VMEM((2 * PEER_HEADS, PEER_NKEYS, tn), F32),
                        pltpu.VMEM((2 * PEER_HEADS, 24, tn), F32),
                        pltpu.VMEM((PEER_HEADS, PEER_NKEYS, tn), F32),
                        pltpu.VMEM((PEER_HEADS, PEER_NKEYS, tn), F32),
                        pltpu.VMEM((PEER_HEADS, 8, tn), F32),
                        pltpu.VMEM((2, PEER_CHUNK, tn), F32),
                        pltpu.VMEM((2, PEER_CHUNK, tn), BF),
                        pltpu.VMEM((D_MODEL, tn), F32)],
        compiler_params=_params(("parallel", "arbitrary")),
        name="peer_final",
    )(h2d, g2, wqt, keys, u_bf, vt_bf, gf)


def _block_diag(w):
    s, g, r, c = w.shape
    on_diag = (jnp.arange(g)[:, None] == jnp.arange(g)[None, :])[None, :, None, :, None]
    return jnp.where(on_diag, w[:, :, :, None, :], 0.0).reshape(s, g * r, g * c)


def _row(v):
    return v.reshape(1, -1).astype(F32)


def kernel(x_prompt, x_sample, mem_prompt, state_ssm_re, state_ssm_im, state_conv, cache_mem_k, cache_mem_v,
           norm1_g, w_in, b_in, s5_log_dt, s5_a_re, s5_a_im, s5_b_re, s5_b_im, s5_c_re, s5_c_im, s5_d,
           s5_glu_w, s5_glu_b, w_s5_proj, conv_dw_w, conv_dw_b, conv_ln_g, conv_ln_b, w_conv_proj, mem_norm_g,
           w_mem_k, w_mem_v, w_xa_proj, w_out, norm2_g, peer_w_q, peer_sub_keys, peer_u, peer_v, final_norm_g):
    assert norm1_g.shape[0] == 1, "single-layer trunk"
    bp, tp, _ = x_prompt.shape
    bs, ts, _ = x_sample.shape
    n_s = bs * ts
    gps = S5_GROUPS // S5_SUPER

    w1 = wg = w_in.reshape(D_MODEL, -1).astype(BF)
    b1 = bg = _row(b_in)

    def per_super(w, r, c):
        return _block_diag(jnp.swapaxes(w.reshape(S5_SUPER, gps, r, c), 2, 3))

    sw = {
        "bmat": jnp.concatenate([per_super(s5_b_re, S5_STATE, S5_GROUP), per_super(s5_b_im, S5_STATE, S5_GROUP)],
                                axis=2),
        "cmat": jnp.concatenate([per_super(s5_c_re, S5_GROUP, S5_STATE), -per_super(s5_c_im, S5_GROUP, S5_STATE)],
                                axis=1).astype(BF),
        "dskip": _row(s5_d),
        "ldt": _row(jnp.repeat(s5_log_dt.reshape(-1), S5_STATE)),
        "are": _row(s5_a_re),
        "aim": _row(s5_a_im),
        "dww": conv_dw_w.reshape(CONV_K, WIDTH).astype(F32),
        "dwb": _row(conv_dw_b),
        "lng": _row(conv_ln_g),
        "lnb": _row(conv_ln_b),
        "gluw": s5_glu_w.reshape(WIDTH, WIDTH).astype(BF),
        "glub": _row(s5_glu_b),
    }
    g1 = _row(norm1_g)
    merge_w = (g1, wg, bg, w_s5_proj.reshape(WIDTH, D_MODEL).astype(BF), w_conv_proj.reshape(WIDTH, D_MODEL).astype(BF),
               w_xa_proj.reshape(WIDTH, D_MODEL).astype(BF), w_out.reshape(D_MODEL, D_MODEL).astype(BF))
    peer_w = (_row(norm2_g), peer_w_q.reshape(D_MODEL, -1).T.astype(BF),
              peer_sub_keys.reshape(2, PEER_NKEYS, PEER_NKEYS).astype(BF),
              peer_u.reshape(PEER_N_EXPERTS, D_MODEL).astype(BF),
              peer_v.reshape(PEER_N_EXPERTS, D_MODEL).T.astype(BF), _row(final_norm_g))

    mk, mv = _memkv(mem_prompt.reshape(bp * N_MEM, D_MODEL), _row(mem_norm_g),
                    w_mem_k.reshape(D_MODEL, WIDTH).astype(BF), w_mem_v.reshape(D_MODEL, WIDTH).astype(BF))
    tt_p = 64
    zeros_st = jnp.zeros((bp, S5_CH), F32)
    q_p, s5_p, cv_p, a_last, re_p, im_p = _seq(x_prompt, bp, tt_p, 512, zeros_st, zeros_st,
                                               jnp.zeros((HIST * bp, WIDTH), F32), g1, w1, b1, sw)
    xa_p = _attn_prompt(q_p, mk.reshape(bp, N_MEM, WIDTH), mv.reshape(bp, N_MEM, WIDTH))
    h_p = _merge(x_prompt.reshape(bp * tp, D_MODEL), s5_p.reshape(bp * tp, WIDTH), cv_p.reshape(bp * tp, WIDTH),
                 xa_p.reshape(bp * tp, WIDTH), *merge_w)
    y_prompt = _peer_final(h_p, *peer_w).reshape(bp, tp, D_MODEL)
    conv_p = jnp.swapaxes(a_last[(tt_p - HIST) * bp:].reshape(HIST, bp, WIDTH), 0, 1)

    xs2 = x_sample.reshape(n_s, D_MODEL)
    st_conv = state_conv.reshape(bs, HIST, WIDTH)
    hist_s = jnp.swapaxes(st_conv, 0, 1).reshape(HIST * bs, WIDTH)
    q_s, s5_s, cv_s, a_s, re_s, im_s = _seq(xs2, bs, ts, 128, state_ssm_re.reshape(bs, S5_CH),
                                            state_ssm_im.reshape(bs, S5_CH), hist_s, g1, w1, b1, sw)
    xa_s = _attn_sample(q_s.reshape(bs, ts, WIDTH).astype(F32),
                        cache_mem_k.reshape(bs, N_MEM * XA_HEADS, XA_HEAD_DIM),
                        cache_mem_v.reshape(bs, N_MEM * XA_HEADS, XA_HEAD_DIM))
    h_s = _merge(xs2, s5_s, cv_s, xa_s.reshape(n_s, WIDTH).astype(BF), *merge_w)
    y_sample = _peer_final(h_s, *peer_w).reshape(bs, ts, D_MODEL)
    conv_s = jnp.concatenate([st_conv[:, ts:], jnp.swapaxes(a_s.reshape(ts, bs, WIDTH), 0, 1)], axis=1)

    st_shape_p = (1, bp, S5_GROUPS, S5_STATE)
    st_shape_s = (1, bs, S5_GROUPS, S5_STATE)
    return (y_prompt, y_sample, re_p.reshape(st_shape_p), im_p.reshape(st_shape_p), conv_p[None],
            mk.reshape(1, bp, N_MEM, XA_HEADS, XA_HEAD_DIM), mv.reshape(1, bp, N_MEM, XA_HEADS, XA_HEAD_DIM),
            re_s.reshape(st_shape_s), im_s.reshape(st_shape_s), conv_s[None])
```

```python
import functools
import math

import jax
import jax.numpy as jnp
from jax import lax
from jax.experimental import pallas as pl
from jax.experimental.pallas import tpu as pltpu

F32 = jnp.float32
BF = jnp.bfloat16

D_MODEL = 1024
WIDTH = 512
S5_GROUPS = 32
S5_GROUP = 16
S5_STATE = 64
S5_CH = S5_GROUPS * S5_STATE
S5_SUPER = 2
CONV_K = 31
HIST = CONV_K - 1
N_MEM = 256
XA_HEADS = 4
XA_HEAD_DIM = 128
PEER_HEADS = 8
PEER_NKEYS = 128
PEER_N_EXPERTS = PEER_NKEYS * PEER_NKEYS
PEER_TOPK = 16
RMS_EPS = 1e-6
LN_EPS = 1e-5

ROW_TILE = 512
PEER_TOKENS = 512
PEER_EXPERTS = 2048
PEER_CHUNK = 512
PEER_ROWS = 16
PEER_LANES = 256
PEER_ACT_LANES = 512
PEER_VROWS = 1024
VMEM_LIMIT = 56 * 1024 * 1024

_NT = (((1,), (1,)), ((), ()))


def _params(sem):
    return pltpu.CompilerParams(dimension_semantics=sem, vmem_limit_bytes=VMEM_LIMIT)


def _rms(x, g):
    return x * lax.rsqrt(jnp.mean(x * x, axis=-1, keepdims=True) + RMS_EPS) * g


def _sigmoid(x):
    return 1.0 / (1.0 + jnp.exp(-x))


def _gelu(x):
    return 0.5 * x * (1.0 + lax.erf(x * (1.0 / math.sqrt(2.0))))


def _const(shape):
    nd = len(shape)
    return pl.BlockSpec(shape, lambda *_: (0,) * nd, pipeline_mode=pl.Buffered(1))


def _memkv_kernel(m_ref, g_ref, wk_ref, wv_ref, k_ref, v_ref):
    m = _rms(m_ref[...], g_ref[...]).astype(BF)
    k_ref[...] = jnp.dot(m, wk_ref[...], preferred_element_type=F32)
    v_ref[...] = jnp.dot(m, wv_ref[...], preferred_element_type=F32)


def _memkv(mem2d, g, wk, wv):
    rows = mem2d.shape[0]
    return pl.pallas_call(
        _memkv_kernel,
        grid=(rows // ROW_TILE,),
        in_specs=[pl.BlockSpec((ROW_TILE, D_MODEL), lambda i: (i, 0)), _const((1, D_MODEL)),
                  _const((D_MODEL, WIDTH)), _const((D_MODEL, WIDTH))],
        out_specs=[pl.BlockSpec((ROW_TILE, WIDTH), lambda i: (i, 0))] * 2,
        out_shape=[jax.ShapeDtypeStruct((rows, WIDTH), F32)] * 2,
        compiler_params=_params(("parallel",)),
        name="mem_kv",
    )(mem2d, g, wk, wv)


def _seq_kernel(nb, tt, cw, n_steps,
                x_ref, g1_ref, w1_ref, b1_ref, p_ref, pt_ref, bmat_ref, cmat_ref, dsk_ref, ldt_ref, are_ref, aim_ref,
                h0r_ref, h0i_ref, hist_ref, dww_ref, dwb_ref, lng_ref, lnb_ref, gluw_ref, glub_ref,
                q_ref, s5_ref, co_ref, a_ref, sre_ref, sim_ref,
                bu_scr, st_scr, par_scr, cb_scr, y_scr, u_scr, bbar_scr):
    i = pl.program_id(0)
    rt = nb * tt
    half = S5_CH // S5_SUPER
    uw = WIDTH // S5_SUPER

    @pl.when(i == 0)
    def _():
        st_scr[0] = h0r_ref[...]
        st_scr[1] = h0i_ref[...]
        cb_scr[0:HIST * nb, :] = hist_ref[...]
        dt = jnp.exp(ldt_ref[...])
        are = are_ref[...]
        aim = aim_ref[...]
        mag = jnp.exp(are * dt)
        lr = mag * jnp.cos(aim * dt)
        li = mag * jnp.sin(aim * dt)
        den = are * are + aim * aim
        nr = lr - 1.0
        par_scr[0:1, :] = lr
        par_scr[1:2, :] = li
        cr = (nr * are + li * aim) / den
        ci = (li * are - nr * aim) / den
        for s in range(S5_SUPER):
            crs = cr[:, half * s:half * (s + 1)]
            cis = ci[:, half * s:half * (s + 1)]
            b_re = bmat_ref[s, :, 0:half].astype(F32)
            b_im = bmat_ref[s, :, half:2 * half].astype(F32)
            bbar_scr[s, :, 0:half] = (b_re * crs - b_im * cis).astype(BF)
            bbar_scr[s, :, half:2 * half] = (b_im * crs + b_re * cis).astype(BF)

    n_bm = _rms(x_ref[...].reshape(rt, D_MODEL), g1_ref[...]).astype(BF)
    n_tm = jnp.dot(p_ref[...], n_bm, preferred_element_type=F32).astype(BF)

    def in_proj(lo, hi, n=n_tm):
        return jnp.dot(n, w1_ref[:, lo:hi], preferred_element_type=F32) + b1_ref[:, lo:hi]

    za = in_proj(WIDTH, 3 * WIDTH)
    a = za[:, :WIDTH] * _sigmoid(za[:, WIDTH:])
    a_ref[...] = a
    cb_scr[HIST * nb:HIST * nb + rt, :] = a

    def proj_u():
        u_scr[...] = in_proj(0, WIDTH)

    def proj_q():
        q_ref[...] = in_proj(3 * WIDTH, 4 * WIDTH, n=n_bm).astype(BF).reshape(q_ref.shape)

    def proj_b(s):
        bu_scr[:, 2 * half * s:2 * half * (s + 1)] = jnp.dot(u_scr[:, uw * s:uw * (s + 1)].astype(BF), bbar_scr[s],
                                                             preferred_element_type=F32)

    pieces = [proj_u, proj_q] + [functools.partial(proj_b, s) for s in range(S5_SUPER)]
    rb = 32
    n_blocks = rt // rb
    for bi in range(n_blocks):
        for pi, piece in enumerate(pieces):
            if (pi * n_blocks) // len(pieces) == bi:
                piece()
        r0 = bi * rb
        acc = jnp.zeros((rb, WIDTH), F32)
        for k in range(CONV_K):
            tap = jnp.concatenate([dww_ref[8 * k:8 * k + 8, :]] * (rb // 8), axis=0)
            acc = acc + tap * cb_scr[r0 + k * nb:r0 + k * nb + rb, :]
        y_scr[r0:r0 + rb, :] = acc
    if n_steps > 1:
        cb_scr[0:HIST * nb, :] = cb_scr[rt:rt + HIST * nb, :]

    yv = y_scr[...] + dwb_ref[...]
    yc = yv - jnp.mean(yv, axis=-1, keepdims=True)
    var = jnp.mean(yc * yc, axis=-1, keepdims=True)
    ln = yc * lax.rsqrt(var + LN_EPS) * lng_ref[...] + lnb_ref[...]
    co_tm = (ln * _sigmoid(ln)).astype(BF)
    co_ref[...] = jnp.dot(pt_ref[...], co_tm, preferred_element_type=F32).astype(BF).reshape(co_ref.shape)

    for s in range(S5_SUPER):
        for c in range(half // cw):
            nat = slice(half * s + c * cw, half * s + (c + 1) * cw)
            re_sl = slice(2 * half * s + c * cw, 2 * half * s + (c + 1) * cw)
            im_sl = slice(2 * half * s + half + c * cw, 2 * half * s + half + (c + 1) * cw)
            lr_b = jnp.broadcast_to(par_scr[0:1, nat], (nb, cw))
            li_b = jnp.broadcast_to(par_scr[1:2, nat], (nb, cw))

            def step(t, carry, re_sl=re_sl, im_sl=im_sl, lr_b=lr_b, li_b=li_b):
                s_re, s_im = carry
                r0 = pl.multiple_of(t * nb, nb)
                n_re = lr_b * s_re - li_b * s_im + bu_scr[pl.ds(r0, nb), re_sl]
                n_im = lr_b * s_im + li_b * s_re + bu_scr[pl.ds(r0, nb), im_sl]
                bu_scr[pl.ds(r0, nb), re_sl] = n_re
                bu_scr[pl.ds(r0, nb), im_sl] = n_im
                return n_re, n_im

            s_re, s_im = lax.fori_loop(0, tt, step, (st_scr[0, :, nat], st_scr[1, :, nat]), unroll=min(tt, 8))
            st_scr[0, :, nat] = s_re
            st_scr[1, :, nat] = s_im

    y = jnp.concatenate(
        [jnp.dot(bu_scr[:, 2 * half * s:2 * half * (s + 1)].astype(BF), cmat_ref[s], preferred_element_type=F32)
         for s in range(S5_SUPER)], axis=-1) + dsk_ref[...] * u_scr[...]
    zz = _gelu(y)
    gate = _sigmoid(jnp.dot(zz.astype(BF), gluw_ref[...], preferred_element_type=F32) + glub_ref[...])
    s5_tm = (zz * gate).astype(BF)
    s5_ref[...] = jnp.dot(pt_ref[...], s5_tm, preferred_element_type=F32).astype(BF).reshape(s5_ref.shape)

    @pl.when(i == n_steps - 1)
    def _():
        sre_ref[...] = st_scr[0]
        sim_ref[...] = st_scr[1]


def _seq(x, nb, tt, cw, h0r, h0i, hist, g1, w1, b1, sw):
    rt = nb * tt
    if x.ndim == 3:
        n_steps = x.shape[1] // tt
        xspec = pl.BlockSpec((nb, tt, D_MODEL), lambda i: (0, i, 0))
        ospec = pl.BlockSpec((nb, tt, WIDTH), lambda i: (0, i, 0))
        oshape = jax.ShapeDtypeStruct((nb, x.shape[1], WIDTH), BF)
    else:
        n_steps = 1
        xspec = pl.BlockSpec((rt, D_MODEL), lambda i: (0, 0))
        ospec = pl.BlockSpec((rt, WIDTH), lambda i: (0, 0))
        oshape = jax.ShapeDtypeStruct((rt, WIDTH), BF)
    idx = jnp.arange(rt)
    tm_of_bm = (idx % tt) * nb + idx // tt
    perm = (idx[:, None] == tm_of_bm[None, :]).astype(BF)
    perm_t = (tm_of_bm[:, None] == idx[None, :]).astype(BF)
    st = _const((nb, S5_CH))
    vec = _const((1, WIDTH))
    chv = _const((1, S5_CH))
    half = S5_CH // S5_SUPER
    return pl.pallas_call(
        functools.partial(_seq_kernel, nb, tt, cw, n_steps),
        grid=(n_steps,),
        in_specs=[xspec, _const((1, D_MODEL)), _const(w1.shape), _const(b1.shape),
                  _const((rt, rt)), _const((rt, rt)),
                  _const((S5_SUPER, WIDTH // S5_SUPER, 2 * half)), _const((S5_SUPER, 2 * half, WIDTH // S5_SUPER)),
                  vec, chv, chv, chv, st, st, _const((HIST * nb, WIDTH)), _const((8 * CONV_K, WIDTH)), vec, vec, vec,
                  _const((WIDTH, WIDTH)), vec],
        out_specs=[ospec, ospec, ospec, _const((rt, WIDTH)), st, st],
        out_shape=[oshape, oshape, oshape, jax.ShapeDtypeStruct((rt, WIDTH), F32),
                   jax.ShapeDtypeStruct((nb, S5_CH), F32), jax.ShapeDtypeStruct((nb, S5_CH), F32)],
        scratch_shapes=[pltpu.VMEM((rt, 2 * S5_CH), F32), pltpu.VMEM((2, nb, S5_CH), F32),
                        pltpu.VMEM((8, S5_CH), F32), pltpu.VMEM(((HIST + tt) * nb, WIDTH), F32),
                        pltpu.VMEM((rt, WIDTH), F32), pltpu.VMEM((rt, WIDTH), F32),
                        pltpu.VMEM((S5_SUPER, WIDTH // S5_SUPER, 2 * half), BF)],
        compiler_params=_params(("arbitrary",)),
        name="s5_conv",
    )(x, g1, w1, b1, perm, perm_t, sw["bmat"], sw["cmat"], sw["dskip"], sw["ldt"], sw["are"], sw["aim"],
      h0r, h0i, hist, sw["dww"], sw["dwb"], sw["lng"], sw["lnb"], sw["gluw"], sw["glub"])


def _attn_heads(q, k, v):
    scale = XA_HEAD_DIM ** -0.5
    outs = []
    for h in range(XA_HEADS):
        sl = slice(h * XA_HEAD_DIM, (h + 1) * XA_HEAD_DIM)
        s = lax.dot_general(q[:, sl], k[:, sl], _NT, preferred_element_type=F32) * scale
        e = jnp.exp(s - jnp.max(s, axis=-1, keepdims=True))
        p = e / jnp.sum(e, axis=-1, keepdims=True)
        outs.append(jnp.dot(p.astype(BF), v[:, sl], preferred_element_type=F32))
    return jnp.concatenate(outs, axis=-1)


def _attn_prompt_kernel(q_ref, k_ref, v_ref, o_ref):
    o_ref[...] = _attn_heads(q_ref[...], k_ref[...].astype(BF), v_ref[...].astype(BF)).astype(o_ref.dtype)


def _attn_prompt(q3, k3, v3):
    bn, tn, _ = q3.shape
    tm = min(ROW_TILE, tn)
    kv = pl.BlockSpec((None, N_MEM, WIDTH), lambda bi, i: (bi, 0, 0))
    qs = pl.BlockSpec((None, tm, WIDTH), lambda bi, i: (bi, i, 0))
    return pl.pallas_call(
        _attn_prompt_kernel,
        grid=(bn, tn // tm),
        in_specs=[qs, kv, kv],
        out_specs=qs,
        out_shape=jax.ShapeDtypeStruct((bn, tn, WIDTH), BF),
        compiler_params=_params(("parallel", "parallel")),
        name="xattn_prompt",
    )(q3, k3, v3)


def _attn_sample_kernel(bb, tq, q_ref, k_ref, v_ref, o_ref):
    nr = XA_HEADS * tq
    nc = N_MEM * XA_HEADS
    row_head = lax.broadcasted_iota(jnp.int32, (nr, nc), 0) // tq
    col_head = lax.broadcasted_iota(jnp.int32, (nr, nc), 1) % XA_HEADS
    own = row_head == col_head
    scale = XA_HEAD_DIM ** -0.5
    for b in range(bb):
        qb = q_ref[b]
        qrows = jnp.concatenate([qb[:, h * XA_HEAD_DIM:(h + 1) * XA_HEAD_DIM] for h in range(XA_HEADS)],
                                axis=0).astype(BF)
        s = lax.dot_general(qrows, k_ref[b].astype(BF), _NT, preferred_element_type=F32) * scale
        s = jnp.where(own, s, -jnp.inf)
        e = jnp.exp(s - jnp.max(s, axis=-1, keepdims=True))
        p = e / jnp.sum(e, axis=-1, keepdims=True)
        of = jnp.dot(p.astype(BF), v_ref[b].astype(BF), preferred_element_type=F32)
        o_ref[b] = jnp.concatenate([of[h * tq:(h + 1) * tq] for h in range(XA_HEADS)], axis=1)


def _attn_sample(q3, k3, v3):
    bn, tq, _ = q3.shape
    bb = 8
    kv = pl.BlockSpec((bb, N_MEM * XA_HEADS, XA_HEAD_DIM), lambda i: (i, 0, 0))
    qs = pl.BlockSpec((bb, tq, WIDTH), lambda i: (i, 0, 0))
    return pl.pallas_call(
        functools.partial(_attn_sample_kernel, bb, tq),
        grid=(bn // bb,),
        in_specs=[qs, kv, kv],
        out_specs=qs,
        out_shape=jax.ShapeDtypeStruct((bn, tq, WIDTH), F32),
        compiler_params=_params(("parallel",)),
        name="xattn_sample",
    )(q3, k3, v3)


def _merge_kernel(x_ref, s5_ref, cv_ref, xa_ref, g_ref, wg_ref, bg_ref, ws5_ref, wcv_ref, wxa_ref, wout_ref, h_ref):
    x = x_ref[...]
    n = _rms(x, g_ref[...]).astype(BF)
    gate = _sigmoid(jnp.dot(n, wg_ref[:, 4 * WIDTH:], preferred_element_type=F32) + bg_ref[:, 4 * WIDTH:])
    merged = (gate[:, :D_MODEL] * jnp.dot(s5_ref[...], ws5_ref[...], preferred_element_type=F32)
              + gate[:, D_MODEL:2 * D_MODEL] * jnp.dot(cv_ref[...], wcv_ref[...], preferred_element_type=F32)
              + gate[:, 2 * D_MODEL:] * jnp.dot(xa_ref[...], wxa_ref[...], preferred_element_type=F32))
    h_ref[...] = x + jnp.dot(merged.astype(BF), wout_ref[...], preferred_element_type=F32)


def _merge(x2, s5, cv, xa, g, wg, bg, ws5, wcv, wxa, wout):
    rows = x2.shape[0]
    xs = pl.BlockSpec((ROW_TILE, D_MODEL), lambda i: (i, 0))
    br = pl.BlockSpec((ROW_TILE, WIDTH), lambda i: (i, 0))
    proj = _const((WIDTH, D_MODEL))
    return pl.pallas_call(
        _merge_kernel,
        grid=(rows // ROW_TILE,),
        in_specs=[xs, br, br, br, _const((1, D_MODEL)), _const(wg.shape), _const(bg.shape),
                  proj, proj, proj, _const((D_MODEL, D_MODEL))],
        out_specs=xs,
        out_shape=jax.ShapeDtypeStruct((rows, D_MODEL), F32),
        compiler_params=_params(("parallel",)),
        name="merge_out",
    )(x2, s5, cv, xa, g, wg, bg, ws5, wcv, wxa, wout)


def _top_values(work, count):
    vals = []
    for _ in range(count):
        m = jnp.max(work, axis=0, keepdims=True)
        vals.append(m)
        work = jnp.where(work == m, -jnp.inf, work)
    return vals


def _sorting_network(n):
    pairs = []

    def merge(lo, cnt, r):
        step = 2 * r
        if step < cnt:
            merge(lo, cnt, step)
            merge(lo + r, cnt, step)
            pairs.extend((k, k + r) for k in range(lo + r, lo + cnt - r, step))
        else:
            pairs.append((lo, lo + r))

    def sort(lo, cnt):
        if cnt > 1:
            sort(lo, cnt // 2)
            sort(lo + cnt // 2, cnt // 2)
            merge(lo, cnt, 1)

    sort(0, n)
    return pairs


def _top_values_sorted(work, count):
    cols = [work[8 * k:8 * (k + 1)] for k in range(work.shape[0] // 8)]
    for lo, hi in _sorting_network(len(cols)):
        cols[lo], cols[hi] = jnp.maximum(cols[lo], cols[hi]), jnp.minimum(cols[lo], cols[hi])
    vals = []
    for k in range(count):
        m = jnp.max(cols[0], axis=0, keepdims=True)
        vals.append(m)
        needed = count - 1 - k
        if needed == 0:
            break
        hit = cols[0] == m
        cols = [jnp.where(hit, cols[d + 1] if d + 1 < len(cols) else -jnp.inf, cols[d])
                for d in range(min(len(cols), needed))]
    return vals


def _top_pair_sums(sv1, sv2, row8, count):
    assert 8 < count < 18
    main = []
    for r in range(count):
        limit = count // (r + 1)
        v = sv1[r:r + 1] + sv2[0:8]
        main.append(v if limit >= 8 else jnp.where(row8 < limit, v, -jnp.inf))
    groups = [main, [sv1[0:1] + sv2[8:16]], [sv1[0:1] + sv2[16:24]]]
    vals = []
    for k in range(count):
        head = groups[0][0]
        for g in groups[1:]:
            head = jnp.maximum(head, g[0])
        m = jnp.max(head, axis=0, keepdims=True)
        vals.append(m)
        needed = count - 1 - k
        if needed == 0:
            break
        shifted = []
        for g in groups:
            hit = g[0] == m
            shifted.append([jnp.where(hit, g[d + 1] if d + 1 < len(g) else -jnp.inf, g[d])
                            for d in range(min(len(g), needed))])
        groups = shifted
    return vals


def _peer_kernel(n_esteps, h_ref, g2_ref, wqt_ref, keys_ref, u_ref, vt_ref, gf_ref, y_ref,
                 x2_scr, qt_scr, st_scr, sv_scr, e1_scr, e2_scr, cz_scr, at_scr, mt_scr, acc_scr):
    j = pl.program_id(1)
    tn = PEER_TOKENS
    nk = PEER_TOPK + 1
    lanes = 128

    @pl.when(j == 0)
    def _():
        x2f = _rms(h_ref[...], g2_ref[...])
        x2 = x2f.astype(BF)
        x2_scr[...] = (x2f * (1.0 / math.sqrt(2.0))).astype(BF)
        qt_scr[...] = lax.dot_general(wqt_ref[...], x2, _NT, preferred_element_type=F32)
        for hp in range(2 * PEER_HEADS):
            st_scr[hp] = jnp.dot(keys_ref[hp % 2], qt_scr[hp * PEER_NKEYS:(hp + 1) * PEER_NKEYS, :].astype(BF),
                                 preferred_element_type=F32)

        def half_body(hp, carry):
            for lt in range(tn // lanes):
                ls = slice(lt * lanes, (lt + 1) * lanes)
                vals = _top_values_sorted(st_scr[hp, :, ls], nk)
                for k in range(nk):
                    sv_scr[hp, k:k + 1, ls] = vals[k]
                sv_scr[hp, nk:24, ls] = jnp.full((24 - nk, lanes), -jnp.inf, F32)
            return carry

        lax.fori_loop(0, 2 * PEER_HEADS, half_body, 0)

        row8 = lax.broadcasted_iota(jnp.int32, (8, lanes), 0)

        def head_body(h, carry):
            for lt in range(tn // lanes):
                ls = slice(lt * lanes, (lt + 1) * lanes)
                sv1 = sv_scr[2 * h, :, ls]
                sv2 = sv_scr[2 * h + 1, :, ls]
                cv = _top_pair_sums(sv1, sv2, row8, nk)
                z = jnp.ones_like(cv[0])
                for k in range(1, PEER_TOPK):
                    z = z + jnp.exp(cv[k] - cv[0])
                tau = 0.5 * (cv[PEER_TOPK - 1] + cv[PEER_TOPK])
                hz = (0.5 * math.sqrt(2.0)) / z
                cz_scr[h, :, ls] = jnp.broadcast_to(jnp.exp(tau - cv[0]) * hz, (8, lanes))
                e1_scr[h, :, ls] = jnp.exp(st_scr[2 * h, :, ls] - sv1[0:1]) * hz
                e2_scr[h, :, ls] = jnp.exp(st_scr[2 * h + 1, :, ls] - sv2[0:1])
            return carry

        lax.fori_loop(0, PEER_HEADS, head_body, 0)
        acc_scr[...] = jnp.zeros_like(acc_scr)

    ch = PEER_CHUNK
    n_chunks = PEER_EXPERTS // ch

    def act_piece(c, nh):
        ns = slice(nh * PEER_ACT_LANES, (nh + 1) * PEER_ACT_LANES)
        at_scr[c % 2, :, ns] = lax.dot_general(u_ref[c * ch:(c + 1) * ch, :], x2_scr[ns, :], _NT,
                                               preferred_element_type=F32)

    def value_piece(c, mq):
        ms = slice(mq * PEER_VROWS, (mq + 1) * PEER_VROWS)
        acc_scr[ms, :] += jnp.dot(vt_ref[ms, c * ch:(c + 1) * ch], mt_scr[c % 2], preferred_element_type=F32)

    def gate_blocks(c):
        blocks = []
        for a in range(ch // PEER_NKEYS):
            for lw in range(tn // PEER_LANES):
                shared = {}

                def block(rb, a=a, lw=lw, shared=shared):
                    ls = slice(lw * PEER_LANES, (lw + 1) * PEER_LANES)
                    if not shared:
                        i1 = j * (PEER_EXPERTS // PEER_NKEYS) + (c * ch) // PEER_NKEYS + a
                        shared["e1"] = [jnp.broadcast_to(e1_scr[h, pl.ds(i1, 1), ls], (PEER_ROWS, PEER_LANES))
                                        for h in range(PEER_HEADS)]
                        shared["cz"] = [jnp.concatenate([cz_scr[h, :, ls]] * (PEER_ROWS // 8), axis=0)
                                        for h in range(PEER_HEADS)]
                    i2 = rb * PEER_ROWS
                    r0 = a * PEER_NKEYS + i2
                    xa = at_scr[c % 2, r0:r0 + PEER_ROWS, ls]
                    act = xa + xa * lax.erf(xa)
                    w = jnp.zeros((PEER_ROWS, PEER_LANES), F32)
                    for h in range(PEER_HEADS):
                        p = shared["e1"][h] * e2_scr[h, i2:i2 + PEER_ROWS, ls]
                        w = w + jnp.where(p >= shared["cz"][h], p, 0.0)
                    mt_scr[c % 2, r0:r0 + PEER_ROWS, ls] = (w * act).astype(BF)

                blocks += [functools.partial(block, rb) for rb in range(PEER_NKEYS // PEER_ROWS)]
        return blocks

    for nh in range(tn // PEER_ACT_LANES):
        act_piece(0, nh)
    for c in range(n_chunks):
        pieces = []
        if c + 1 < n_chunks:
            pieces += [functools.partial(act_piece, c + 1, nh) for nh in range(tn // PEER_ACT_LANES)]
        if c >= 1:
            pieces += [functools.partial(value_piece, c - 1, mq) for mq in range(D_MODEL // PEER_VROWS)]
        blocks = gate_blocks(c)
        for bi, blk in enumerate(blocks):
            for pi, piece in enumerate(pieces):
                if (pi * len(blocks)) // len(pieces) == bi:
                    piece()
            blk()
    for mq in range(D_MODEL // PEER_VROWS):
        value_piece(n_chunks - 1, mq)

    @pl.when(j == n_esteps - 1)
    def _():
        h2 = h_ref[...] + acc_scr[...].T
        y_ref[...] = _rms(h2, gf_ref[...])


def _peer_final(h2d, g2, wqt, keys, u_bf, vt_bf, gf):
    rows = h2d.shape[0]
    tn = PEER_TOKENS
    n_esteps = PEER_N_EXPERTS // PEER_EXPERTS
    hs = pl.BlockSpec((tn, D_MODEL), lambda i, j: (i, 0))
    return pl.pallas_call(
        functools.partial(_peer_kernel, n_esteps),
        grid=(rows // tn, n_esteps),
        in_specs=[hs, _const((1, D_MODEL)), _const((2 * PEER_HEADS * PEER_NKEYS, D_MODEL)),
                  _const((2, PEER_NKEYS, PEER_NKEYS)),
                  pl.BlockSpec((PEER_EXPERTS, D_MODEL), lambda i, j: (j, 0)),
                  pl.BlockSpec((D_MODEL, PEER_EXPERTS), lambda i, j: (0, j)), _const((1, D_MODEL))],
        out_specs=hs,
        out_shape=jax.ShapeDtypeStruct((rows, D_MODEL), F32),
        scratch_shapes=[pltpu.VMEM((tn, D_MODEL), BF),
                        pltpu.VMEM((2 * PEER_HEADS * PEER_NKEYS, tn), F32),
                        pltpu.VMEM((2 * PEER_HEADS, PEER_NKEYS, tn), F32),
                        pltpu.VMEM((2 * PEER_HEADS, 24, tn), F32),
                        pltpu.VMEM((PEER_HEADS, PEER_NKEYS, tn), F32),
                        pltpu.VMEM((PEER_HEADS, PEER_NKEYS, tn), F32),
                        pltpu.VMEM((PEER_HEADS, 8, tn), F32),
                        pltpu.VMEM((2, PEER_CHUNK, tn), F32),
                        pltpu.VMEM((2, PEER_CHUNK, tn), BF),
                        pltpu.VMEM((D_MODEL, tn), F32)],
        compiler_params=_params(("parallel", "arbitrary")),
        name="peer_final",
    )(h2d, g2, wqt, keys, u_bf, vt_bf, gf)


def _block_diag(w):
    s, g, r, c = w.shape
    on_diag = (jnp.arange(g)[:, None] == jnp.arange(g)[None, :])[None, :, None, :, None]
    return jnp.where(on_diag, w[:, :, :, None, :], 0.0).reshape(s, g * r, g * c)


def _row(v):
    return v.reshape(1, -1).astype(F32)


def kernel(x_prompt, x_sample, mem_prompt, state_ssm_re, state_ssm_im, state_conv, cache_mem_k, cache_mem_v,
           norm1_g, w_in, b_in, s5_log_dt, s5_a_re, s5_a_im, s5_b_re, s5_b_im, s5_c_re, s5_c_im, s5_d,
           s5_glu_w, s5_glu_b, w_s5_proj, conv_dw_w, conv_dw_b, conv_ln_g, conv_ln_b, w_conv_proj, mem_norm_g,
           w_mem_k, w_mem_v, w_xa_proj, w_out, norm2_g, peer_w_q, peer_sub_keys, peer_u, peer_v, final_norm_g):
    assert norm1_g.shape[0] == 1, "single-layer trunk"
    bp, tp, _ = x_prompt.shape
    bs, ts, _ = x_sample.shape
    n_s = bs * ts
    gps = S5_GROUPS // S5_SUPER

    w1 = wg = w_in.reshape(D_MODEL, -1).astype(BF)
    b1 = bg = _row(b_in)

    def per_super(w, r, c):
        return _block_diag(jnp.swapaxes(w.reshape(S5_SUPER, gps, r, c), 2, 3))

    sw = {
        "bmat": jnp.concatenate([per_super(s5_b_re, S5_STATE, S5_GROUP), per_super(s5_b_im, S5_STATE, S5_GROUP)],
                                axis=2),
        "cmat": jnp.concatenate([per_super(s5_c_re, S5_GROUP, S5_STATE), -per_super(s5_c_im, S5_GROUP, S5_STATE)],
                                axis=1).astype(BF),
        "dskip": _row(s5_d),
        "ldt": _row(jnp.repeat(s5_log_dt.reshape(-1), S5_STATE)),
        "are": _row(s5_a_re),
        "aim": _row(s5_a_im),
        "dww": jnp.repeat(conv_dw_w.reshape(CONV_K, WIDTH).astype(F32), 8, axis=0),
        "dwb": _row(conv_dw_b),
        "lng": _row(conv_ln_g),
        "lnb": _row(conv_ln_b),
        "gluw": s5_glu_w.reshape(WIDTH, WIDTH).astype(BF),
        "glub": _row(s5_glu_b),
    }
    g1 = _row(norm1_g)
    merge_w = (g1, wg, bg, w_s5_proj.reshape(WIDTH, D_MODEL).astype(BF), w_conv_proj.reshape(WIDTH, D_MODEL).astype(BF),
               w_xa_proj.reshape(WIDTH, D_MODEL).astype(BF), w_out.reshape(D_MODEL, D_MODEL).astype(BF))
    peer_w = (_row(norm2_g), peer_w_q.reshape(D_MODEL, -1).T.astype(BF),
              peer_sub_keys.reshape(2, PEER_NKEYS, PEER_NKEYS).astype(BF),
              peer_u.reshape(PEER_N_EXPERTS, D_MODEL).astype(BF),
              peer_v.reshape(PEER_N_EXPERTS, D_MODEL).T.astype(BF), _row(final_norm_g))

    mk, mv = _memkv(mem_prompt.reshape(bp * N_MEM, D_MODEL), _row(mem_norm_g),
                    w_mem_k.reshape(D_MODEL, WIDTH).astype(BF), w_mem_v.reshape(D_MODEL, WIDTH).astype(BF))
    tt_p = 64
    zeros_st = jnp.zeros((bp, S5_CH), F32)
    q_p, s5_p, cv_p, a_last, re_p, im_p = _seq(x_prompt, bp, tt_p, 512, zeros_st, zeros_st,
                                               jnp.zeros((HIST * bp, WIDTH), F32), g1, w1, b1, sw)
    xa_p = _attn_prompt(q_p, mk.reshape(bp, N_MEM, WIDTH), mv.reshape(bp, N_MEM, WIDTH))
    h_p = _merge(x_prompt.reshape(bp * tp, D_MODEL), s5_p.reshape(bp * tp, WIDTH), cv_p.reshape(bp * tp, WIDTH),
                 xa_p.reshape(bp * tp, WIDTH), *merge_w)
    y_prompt = _peer_final(h_p, *peer_w).reshape(bp, tp, D_MODEL)
    conv_p = jnp.swapaxes(a_last[(tt_p - HIST) * bp:].reshape(HIST, bp, WIDTH), 0, 1)

    xs2 = x_sample.reshape(n_s, D_MODEL)
    st_conv = state_conv.reshape(bs, HIST, WIDTH)
    hist_s = jnp.swapaxes(st_conv, 0, 1).reshape(HIST * bs, WIDTH)
    q_s, s5_s, cv_s, a_s, re_s, im_s = _seq(xs2, bs, ts, 128, state_ssm_re.reshape(bs, S5_CH),
                                            state_ssm_im.reshape(bs, S5_CH), hist_s, g1, w1, b1, sw)
    xa_s = _attn_sample(q_s.reshape(bs, ts, WIDTH).astype(F32),
                        cache_mem_k.reshape(bs, N_MEM * XA_HEADS, XA_HEAD_DIM),
                        cache_mem_v.reshape(bs, N_MEM * XA_HEADS, XA_HEAD_DIM))
    h_s = _merge(xs2, s5_s, cv_s, xa_s.reshape(n_s, WIDTH).astype(BF), *merge_w)
    y_sample = _peer_final(h_s, *peer_w).reshape(bs, ts, D_MODEL)
    conv_s = jnp.concatenate([st_conv[:, ts:], jnp.swapaxes(a_s.reshape(ts, bs, WIDTH), 0, 1)], axis=1)

    st_shape_p = (1, bp, S5_GROUPS, S5_STATE)
    st_shape_s = (1, bs, S5_GROUPS, S5_STATE)
    return (y_prompt, y_sample, re_p.reshape(st_shape_p), im_p.reshape(st_shape_p), conv_p[None],
            mk.reshape(1, bp, N_MEM, XA_HEADS, XA_HEAD_DIM), mv.reshape(1, bp, N_MEM, XA_HEADS, XA_HEAD_DIM),
            re_s.reshape(st_shape_s), im_s.reshape(st_shape_s), conv_s[None])
```

```python
import functools
import math

import jax
import jax.numpy as jnp
from jax import lax
from jax.experimental import pallas as pl
from jax.experimental.pallas import tpu as pltpu

F32 = jnp.float32
BF = jnp.bfloat16

SUBLANES = 8
LANES = 128

D_MODEL = 1024
WIDTH = 512
S5_GROUPS = 32
S5_GROUP = 16
S5_STATE = 64
S5_CH = S5_GROUPS * S5_STATE
S5_SUPER = 2
CONV_K = 31
HIST = CONV_K - 1
N_MEM = 256
XA_HEADS = 4
XA_HEAD_DIM = 128
PEER_HEADS = 8
PEER_NKEYS = 128
PEER_N_EXPERTS = PEER_NKEYS * PEER_NKEYS
PEER_TOPK = 16
PEER_SV_ROWS = -(-(PEER_TOPK + 1) // SUBLANES) * SUBLANES
RMS_EPS = 1e-6
LN_EPS = 1e-5

ROW_TILE = 512
PEER_TOKENS = 512
PEER_EXPERTS = 2048
PEER_CHUNK = 512
PEER_ROWS = 16
PEER_LANES = 256
PEER_ACT_LANES = 512
PEER_VROWS = 1024
VMEM_LIMIT = 56 * 1024 * 1024

_NT = (((1,), (1,)), ((), ()))


def _params(sem):
    return pltpu.CompilerParams(dimension_semantics=sem, vmem_limit_bytes=VMEM_LIMIT)


def _rms(x, g):
    return x * lax.rsqrt(jnp.mean(x * x, axis=-1, keepdims=True) + RMS_EPS) * g


def _sigmoid(x):
    return 1.0 / (1.0 + jnp.exp(-x))


def _gelu(x):
    return 0.5 * x * (1.0 + lax.erf(x * (1.0 / math.sqrt(2.0))))


def _const(shape):
    nd = len(shape)
    return pl.BlockSpec(shape, lambda *_: (0,) * nd, pipeline_mode=pl.Buffered(1))


def _memkv_kernel(m_ref, g_ref, wk_ref, wv_ref, k_ref, v_ref):
    m = _rms(m_ref[...], g_ref[...]).astype(BF)
    k_ref[...] = jnp.dot(m, wk_ref[...], preferred_element_type=F32)
    v_ref[...] = jnp.dot(m, wv_ref[...], preferred_element_type=F32)


def _memkv(mem2d, g, wk, wv):
    rows = mem2d.shape[0]
    return pl.pallas_call(
        _memkv_kernel,
        grid=(rows // ROW_TILE,),
        in_specs=[pl.BlockSpec((ROW_TILE, D_MODEL), lambda i: (i, 0)), _const((1, D_MODEL)),
                  _const((D_MODEL, WIDTH)), _const((D_MODEL, WIDTH))],
        out_specs=[pl.BlockSpec((ROW_TILE, WIDTH), lambda i: (i, 0))] * 2,
        out_shape=[jax.ShapeDtypeStruct((rows, WIDTH), F32)] * 2,
        compiler_params=_params(("parallel",)),
        name="mem_kv",
    )(mem2d, g, wk, wv)


def _seq_kernel(nb, tt, cw, n_steps,
                x_ref, g1_ref, w1_ref, b1_ref, p_ref, pt_ref, bmat_ref, cmat_ref, dsk_ref, ldt_ref, are_ref, aim_ref,
                h0r_ref, h0i_ref, hist_ref, dww_ref, dwb_ref, lng_ref, lnb_ref, gluw_ref, glub_ref,
                q_ref, s5_ref, co_ref, a_ref, sre_ref, sim_ref,
                bu_scr, st_scr, par_scr, cb_scr, y_scr, u_scr, bbar_scr):
    i = pl.program_id(0)
    rt = nb * tt
    half = S5_CH // S5_SUPER
    uw = WIDTH // S5_SUPER

    @pl.when(i == 0)
    def _():
        st_scr[0] = h0r_ref[...]
        st_scr[1] = h0i_ref[...]
        cb_scr[0:HIST * nb, :] = hist_ref[...]
        dt = jnp.exp(ldt_ref[...])
        are = are_ref[...]
        aim = aim_ref[...]
        mag = jnp.exp(are * dt)
        lr = mag * jnp.cos(aim * dt)
        li = mag * jnp.sin(aim * dt)
        den = are * are + aim * aim
        nr = lr - 1.0
        par_scr[0:1, :] = lr
        par_scr[1:2, :] = li
        cr = (nr * are + li * aim) / den
        ci = (li * are - nr * aim) / den
        for s in range(S5_SUPER):
            crs = cr[:, half * s:half * (s + 1)]
            cis = ci[:, half * s:half * (s + 1)]
            b_re = bmat_ref[s, :, 0:half].astype(F32)
            b_im = bmat_ref[s, :, half:2 * half].astype(F32)
            bbar_scr[s, :, 0:half] = (b_re * crs - b_im * cis).astype(BF)
            bbar_scr[s, :, half:2 * half] = (b_im * crs + b_re * cis).astype(BF)

    n_bm = _rms(x_ref[...].reshape(rt, D_MODEL), g1_ref[...]).astype(BF)
    n_tm = jnp.dot(p_ref[...], n_bm, preferred_element_type=F32).astype(BF)

    def in_proj(lo, hi, n=n_tm):
        return jnp.dot(n, w1_ref[:, lo:hi], preferred_element_type=F32) + b1_ref[:, lo:hi]

    za = in_proj(WIDTH, 3 * WIDTH)
    a = za[:, :WIDTH] * _sigmoid(za[:, WIDTH:])
    a_ref[...] = a
    cb_scr[HIST * nb:HIST * nb + rt, :] = a

    def proj_u():
        u_scr[...] = in_proj(0, WIDTH)

    def proj_q():
        q_ref[...] = in_proj(3 * WIDTH, 4 * WIDTH, n=n_bm).astype(BF).reshape(q_ref.shape)

    def proj_b(s):
        bu_scr[:, 2 * half * s:2 * half * (s + 1)] = jnp.dot(u_scr[:, uw * s:uw * (s + 1)].astype(BF), bbar_scr[s],
                                                             preferred_element_type=F32)

    pieces = [proj_u, proj_q] + [functools.partial(proj_b, s) for s in range(S5_SUPER)]
    rb = 32
    n_blocks = rt // rb
    for bi in range(n_blocks):
        for pi, piece in enumerate(pieces):
            if (pi * n_blocks) // len(pieces) == bi:
                piece()
        r0 = bi * rb
        acc = jnp.zeros((rb, WIDTH), F32)
        for k in range(CONV_K):
            tap = jnp.concatenate([dww_ref[SUBLANES * k:SUBLANES * (k + 1), :]] * (rb // SUBLANES), axis=0)
            acc = acc + tap * cb_scr[r0 + k * nb:r0 + k * nb + rb, :]
        y_scr[r0:r0 + rb, :] = acc
    if n_steps > 1:
        cb_scr[0:HIST * nb, :] = cb_scr[rt:rt + HIST * nb, :]

    yv = y_scr[...] + dwb_ref[...]
    yc = yv - jnp.mean(yv, axis=-1, keepdims=True)
    var = jnp.mean(yc * yc, axis=-1, keepdims=True)
    ln = yc * lax.rsqrt(var + LN_EPS) * lng_ref[...] + lnb_ref[...]
    co_tm = (ln * _sigmoid(ln)).astype(BF)
    co_ref[...] = jnp.dot(pt_ref[...], co_tm, preferred_element_type=F32).astype(BF).reshape(co_ref.shape)

    for s in range(S5_SUPER):
        for c in range(half // cw):
            nat = slice(half * s + c * cw, half * s + (c + 1) * cw)
            re_sl = slice(2 * half * s + c * cw, 2 * half * s + (c + 1) * cw)
            im_sl = slice(2 * half * s + half + c * cw, 2 * half * s + half + (c + 1) * cw)
            lr_b = jnp.broadcast_to(par_scr[0:1, nat], (nb, cw))
            li_b = jnp.broadcast_to(par_scr[1:2, nat], (nb, cw))

            def step(t, carry, re_sl=re_sl, im_sl=im_sl, lr_b=lr_b, li_b=li_b):
                s_re, s_im = carry
                r0 = pl.multiple_of(t * nb, nb)
                n_re = lr_b * s_re - li_b * s_im + bu_scr[pl.ds(r0, nb), re_sl]
                n_im = lr_b * s_im + li_b * s_re + bu_scr[pl.ds(r0, nb), im_sl]
                bu_scr[pl.ds(r0, nb), re_sl] = n_re
                bu_scr[pl.ds(r0, nb), im_sl] = n_im
                return n_re, n_im

            s_re, s_im = lax.fori_loop(0, tt, step, (st_scr[0, :, nat], st_scr[1, :, nat]), unroll=min(tt, 8))
            st_scr[0, :, nat] = s_re
            st_scr[1, :, nat] = s_im

    y = jnp.concatenate(
        [jnp.dot(bu_scr[:, 2 * half * s:2 * half * (s + 1)].astype(BF), cmat_ref[s], preferred_element_type=F32)
         for s in range(S5_SUPER)], axis=-1) + dsk_ref[...] * u_scr[...]
    zz = _gelu(y)
    gate = _sigmoid(jnp.dot(zz.astype(BF), gluw_ref[...], preferred_element_type=F32) + glub_ref[...])
    s5_tm = (zz * gate).astype(BF)
    s5_ref[...] = jnp.dot(pt_ref[...], s5_tm, preferred_element_type=F32).astype(BF).reshape(s5_ref.shape)

    @pl.when(i == n_steps - 1)
    def _():
        sre_ref[...] = st_scr[0]
        sim_ref[...] = st_scr[1]


def _seq(x, nb, tt, cw, h0r, h0i, hist, g1, w1, b1, sw):
    rt = nb * tt
    if x.ndim == 3:
        n_steps = x.shape[1] // tt
        xspec = pl.BlockSpec((nb, tt, D_MODEL), lambda i: (0, i, 0))
        ospec = pl.BlockSpec((nb, tt, WIDTH), lambda i: (0, i, 0))
        oshape = jax.ShapeDtypeStruct((nb, x.shape[1], WIDTH), BF)
    else:
        n_steps = 1
        xspec = pl.BlockSpec((rt, D_MODEL), lambda i: (0, 0))
        ospec = pl.BlockSpec((rt, WIDTH), lambda i: (0, 0))
        oshape = jax.ShapeDtypeStruct((rt, WIDTH), BF)
    idx = jnp.arange(rt)
    tm_of_bm = (idx % tt) * nb + idx // tt
    perm = (idx[:, None] == tm_of_bm[None, :]).astype(BF)
    perm_t = (tm_of_bm[:, None] == idx[None, :]).astype(BF)
    st = _const((nb, S5_CH))
    vec = _const((1, WIDTH))
    chv = _const((1, S5_CH))
    half = S5_CH // S5_SUPER
    return pl.pallas_call(
        functools.partial(_seq_kernel, nb, tt, cw, n_steps),
        grid=(n_steps,),
        in_specs=[xspec, _const((1, D_MODEL)), _const(w1.shape), _const(b1.shape),
                  _const((rt, rt)), _const((rt, rt)),
                  _const((S5_SUPER, WIDTH // S5_SUPER, 2 * half)), _const((S5_SUPER, 2 * half, WIDTH // S5_SUPER)),
                  vec, chv, chv, chv, st, st, _const((HIST * nb, WIDTH)), _const((SUBLANES * CONV_K, WIDTH)), vec, vec, vec,
                  _const((WIDTH, WIDTH)), vec],
        out_specs=[ospec, ospec, ospec, _const((rt, WIDTH)), st, st],
        out_shape=[oshape, oshape, oshape, jax.ShapeDtypeStruct((rt, WIDTH), F32),
                   jax.ShapeDtypeStruct((nb, S5_CH), F32), jax.ShapeDtypeStruct((nb, S5_CH), F32)],
        scratch_shapes=[pltpu.VMEM((rt, 2 * S5_CH), F32), pltpu.VMEM((2, nb, S5_CH), F32),
                        pltpu.VMEM((SUBLANES, S5_CH), F32), pltpu.VMEM(((HIST + tt) * nb, WIDTH), F32),
                        pltpu.VMEM((rt, WIDTH), F32), pltpu.VMEM((rt, WIDTH), F32),
                        pltpu.VMEM((S5_SUPER, WIDTH // S5_SUPER, 2 * half), BF)],
        compiler_params=_params(("arbitrary",)),
        name="s5_conv",
    )(x, g1, w1, b1, perm, perm_t, sw["bmat"], sw["cmat"], sw["dskip"], sw["ldt"], sw["are"], sw["aim"],
      h0r, h0i, hist, sw["dww"], sw["dwb"], sw["lng"], sw["lnb"], sw["gluw"], sw["glub"])


def _attn_heads(q, k, v):
    scale = XA_HEAD_DIM ** -0.5
    outs = []
    for h in range(XA_HEADS):
        sl = slice(h * XA_HEAD_DIM, (h + 1) * XA_HEAD_DIM)
        s = lax.dot_general(q[:, sl], k[:, sl], _NT, preferred_element_type=F32) * scale
        e = jnp.exp(s - jnp.max(s, axis=-1, keepdims=True))
        p = e / jnp.sum(e, axis=-1, keepdims=True)
        outs.append(jnp.dot(p.astype(BF), v[:, sl], preferred_element_type=F32))
    return jnp.concatenate(outs, axis=-1)


def _attn_prompt_kernel(q_ref, k_ref, v_ref, o_ref):
    o_ref[...] = _attn_heads(q_ref[...], k_ref[...].astype(BF), v_ref[...].astype(BF)).astype(o_ref.dtype)


def _attn_prompt(q3, k3, v3):
    bn, tn, _ = q3.shape
    tm = min(ROW_TILE, tn)
    kv = pl.BlockSpec((None, N_MEM, WIDTH), lambda bi, i: (bi, 0, 0))
    qs = pl.BlockSpec((None, tm, WIDTH), lambda bi, i: (bi, i, 0))
    return pl.pallas_call(
        _attn_prompt_kernel,
        grid=(bn, tn // tm),
        in_specs=[qs, kv, kv],
        out_specs=qs,
        out_shape=jax.ShapeDtypeStruct((bn, tn, WIDTH), BF),
        compiler_params=_params(("parallel", "parallel")),
        name="xattn_prompt",
    )(q3, k3, v3)


def _attn_sample_kernel(bb, tq, q_ref, k_ref, v_ref, o_ref):
    nr = XA_HEADS * tq
    nc = N_MEM * XA_HEADS
    row_head = lax.broadcasted_iota(jnp.int32, (nr, nc), 0) // tq
    col_head = lax.broadcasted_iota(jnp.int32, (nr, nc), 1) % XA_HEADS
    own = row_head == col_head
    scale = XA_HEAD_DIM ** -0.5
    for b in range(bb):
        qb = q_ref[b]
        qrows = jnp.concatenate([qb[:, h * XA_HEAD_DIM:(h + 1) * XA_HEAD_DIM] for h in range(XA_HEADS)],
                                axis=0).astype(BF)
        s = lax.dot_general(qrows, k_ref[b].astype(BF), _NT, preferred_element_type=F32) * scale
        s = jnp.where(own, s, -jnp.inf)
        e = jnp.exp(s - jnp.max(s, axis=-1, keepdims=True))
        p = e / jnp.sum(e, axis=-1, keepdims=True)
        of = jnp.dot(p.astype(BF), v_ref[b].astype(BF), preferred_element_type=F32)
        o_ref[b] = jnp.concatenate([of[h * tq:(h + 1) * tq] for h in range(XA_HEADS)], axis=1)


def _attn_sample(q3, k3, v3):
    bn, tq, _ = q3.shape
    bb = 8
    kv = pl.BlockSpec((bb, N_MEM * XA_HEADS, XA_HEAD_DIM), lambda i: (i, 0, 0))
    qs = pl.BlockSpec((bb, tq, WIDTH), lambda i: (i, 0, 0))
    return pl.pallas_call(
        functools.partial(_attn_sample_kernel, bb, tq),
        grid=(bn // bb,),
        in_specs=[qs, kv, kv],
        out_specs=qs,
        out_shape=jax.ShapeDtypeStruct((bn, tq, WIDTH), F32),
        compiler_params=_params(("parallel",)),
        name="xattn_sample",
    )(q3, k3, v3)


def _merge_kernel(x_ref, s5_ref, cv_ref, xa_ref, g_ref, wg_ref, bg_ref, ws5_ref, wcv_ref, wxa_ref, wout_ref, h_ref):
    x = x_ref[...]
    n = _rms(x, g_ref[...]).astype(BF)
    gate = _sigmoid(jnp.dot(n, wg_ref[:, 4 * WIDTH:], preferred_element_type=F32) + bg_ref[:, 4 * WIDTH:])
    merged = (gate[:, :D_MODEL] * jnp.dot(s5_ref[...], ws5_ref[...], preferred_element_type=F32)
              + gate[:, D_MODEL:2 * D_MODEL] * jnp.dot(cv_ref[...], wcv_ref[...], preferred_element_type=F32)
              + gate[:, 2 * D_MODEL:] * jnp.dot(xa_ref[...], wxa_ref[...], preferred_element_type=F32))
    h_ref[...] = x + jnp.dot(merged.astype(BF), wout_ref[...], preferred_element_type=F32)


def _merge(x2, s5, cv, xa, g, wg, bg, ws5, wcv, wxa, wout):
    rows = x2.shape[0]
    xs = pl.BlockSpec((ROW_TILE, D_MODEL), lambda i: (i, 0))
    br = pl.BlockSpec((ROW_TILE, WIDTH), lambda i: (i, 0))
    proj = _const((WIDTH, D_MODEL))
    return pl.pallas_call(
        _merge_kernel,
        grid=(rows // ROW_TILE,),
        in_specs=[xs, br, br, br, _const((1, D_MODEL)), _const(wg.shape), _const(bg.shape),
                  proj, proj, proj, _const((D_MODEL, D_MODEL))],
        out_specs=xs,
        out_shape=jax.ShapeDtypeStruct((rows, D_MODEL), F32),
        compiler_params=_params(("parallel",)),
        name="merge_out",
    )(x2, s5, cv, xa, g, wg, bg, ws5, wcv, wxa, wout)


def _top_values(work, count):
    vals = []
    for _ in range(count):
        m = jnp.max(work, axis=0, keepdims=True)
        vals.append(m)
        work = jnp.where(work == m, -jnp.inf, work)
    return vals


def _sorting_network(n):
    pairs = []

    def merge(lo, cnt, r):
        step = 2 * r
        if step < cnt:
            merge(lo, cnt, step)
            merge(lo + r, cnt, step)
            pairs.extend((k, k + r) for k in range(lo + r, lo + cnt - r, step))
        else:
            pairs.append((lo, lo + r))

    def sort(lo, cnt):
        if cnt > 1:
            sort(lo, cnt // 2)
            sort(lo + cnt // 2, cnt // 2)
            merge(lo, cnt, 1)

    sort(0, n)
    return pairs


def _top_values_sorted(work, count):
    cols = [work[SUBLANES * k:SUBLANES * (k + 1)] for k in range(work.shape[0] // SUBLANES)]
    for lo, hi in _sorting_network(len(cols)):
        cols[lo], cols[hi] = jnp.maximum(cols[lo], cols[hi]), jnp.minimum(cols[lo], cols[hi])
    vals = []
    for k in range(count):
        m = jnp.max(cols[0], axis=0, keepdims=True)
        vals.append(m)
        needed = count - 1 - k
        if needed == 0:
            break
        hit = cols[0] == m
        cols = [jnp.where(hit, cols[d + 1] if d + 1 < len(cols) else -jnp.inf, cols[d])
                for d in range(min(len(cols), needed))]
    return vals


def _top_pair_sums(sv1, sv2, row8, count):
    assert SUBLANES < count < 2 * (SUBLANES + 1)
    main = []
    for r in range(count):
        limit = count // (r + 1)
        v = sv1[r:r + 1] + sv2[0:SUBLANES]
        main.append(v if limit >= SUBLANES else jnp.where(row8 < limit, v, -jnp.inf))
    groups = [main] + [[sv1[0:1] + sv2[lo:lo + SUBLANES]] for lo in range(SUBLANES, sv2.shape[0], SUBLANES)]
    vals = []
    for k in range(count):
        head = groups[0][0]
        for g in groups[1:]:
            head = jnp.maximum(head, g[0])
        m = jnp.max(head, axis=0, keepdims=True)
        vals.append(m)
        needed = count - 1 - k
        if needed == 0:
            break
        shifted = []
        for g in groups:
            hit = g[0] == m
            shifted.append([jnp.where(hit, g[d + 1] if d + 1 < len(g) else -jnp.inf, g[d])
                            for d in range(min(len(g), needed))])
        groups = shifted
    return vals


def _peer_kernel(n_esteps, h_ref, g2_ref, wqt_ref, keys_ref, u_ref, vt_ref, gf_ref, y_ref,
                 x2_scr, qt_scr, st_scr, sv_scr, e1_scr, e2_scr, cz_scr, at_scr, mt_scr, acc_scr):
    j = pl.program_id(1)
    tn = PEER_TOKENS
    nk = PEER_TOPK + 1
    lanes = LANES

    @pl.when(j == 0)
    def _():
        x2f = _rms(h_ref[...], g2_ref[...])
        x2 = x2f.astype(BF)
        x2_scr[...] = (x2f * (1.0 / math.sqrt(2.0))).astype(BF)
        qt_scr[...] = lax.dot_general(wqt_ref[...], x2, _NT, preferred_element_type=F32)
        for hp in range(2 * PEER_HEADS):
            st_scr[hp] = jnp.dot(keys_ref[hp % 2], qt_scr[hp * PEER_NKEYS:(hp + 1) * PEER_NKEYS, :].astype(BF),
                                 preferred_element_type=F32)

        def half_body(hp, carry):
            for lt in range(tn // lanes):
                ls = slice(lt * lanes, (lt + 1) * lanes)
                vals = _top_values_sorted(st_scr[hp, :, ls], nk)
                for k in range(nk):
                    sv_scr[hp, k:k + 1, ls] = vals[k]
                sv_scr[hp, nk:PEER_SV_ROWS, ls] = jnp.full((PEER_SV_ROWS - nk, lanes), -jnp.inf, F32)
            return carry

        lax.fori_loop(0, 2 * PEER_HEADS, half_body, 0)

        row8 = lax.broadcasted_iota(jnp.int32, (SUBLANES, lanes), 0)

        def head_body(h, carry):
            for lt in range(tn // lanes):
                ls = slice(lt * lanes, (lt + 1) * lanes)
                sv1 = sv_scr[2 * h, :, ls]
                sv2 = sv_scr[2 * h + 1, :, ls]
                cv = _top_pair_sums(sv1, sv2, row8, nk)
                z = jnp.ones_like(cv[0])
                for k in range(1, PEER_TOPK):
                    z = z + jnp.exp(cv[k] - cv[0])
                tau = 0.5 * (cv[PEER_TOPK - 1] + cv[PEER_TOPK])
                hz = (0.5 * math.sqrt(2.0)) / z
                cz_scr[h, :, ls] = jnp.broadcast_to(jnp.exp(tau - cv[0]) * hz, (SUBLANES, lanes))
                e1_scr[h, :, ls] = jnp.exp(st_scr[2 * h, :, ls] - sv1[0:1]) * hz
                e2_scr[h, :, ls] = jnp.exp(st_scr[2 * h + 1, :, ls] - sv2[0:1])
            return carry

        lax.fori_loop(0, PEER_HEADS, head_body, 0)
        acc_scr[...] = jnp.zeros_like(acc_scr)

    ch = PEER_CHUNK
    n_chunks = PEER_EXPERTS // ch

    def act_piece(c, nh):
        ns = slice(nh * PEER_ACT_LANES, (nh + 1) * PEER_ACT_LANES)
        at_scr[c % 2, :, ns] = lax.dot_general(u_ref[c * ch:(c + 1) * ch, :], x2_scr[ns, :], _NT,
                                               preferred_element_type=F32)

    def value_piece(c, mq):
        ms = slice(mq * PEER_VROWS, (mq + 1) * PEER_VROWS)
        acc_scr[ms, :] += jnp.dot(vt_ref[ms, c * ch:(c + 1) * ch], mt_scr[c % 2], preferred_element_type=F32)

    def gate_blocks(c):
        blocks = []
        for a in range(ch // PEER_NKEYS):
            for lw in range(tn // PEER_LANES):
                shared = {}

                def block(rb, a=a, lw=lw, shared=shared):
                    ls = slice(lw * PEER_LANES, (lw + 1) * PEER_LANES)
                    if not shared:
                        i1 = j * (PEER_EXPERTS // PEER_NKEYS) + (c * ch) // PEER_NKEYS + a
                        shared["e1"] = [jnp.broadcast_to(e1_scr[h, pl.ds(i1, 1), ls], (PEER_ROWS, PEER_LANES))
                                        for h in range(PEER_HEADS)]
                        shared["cz"] = [jnp.concatenate([cz_scr[h, :, ls]] * (PEER_ROWS // SUBLANES), axis=0)
                                        for h in range(PEER_HEADS)]
                    i2 = rb * PEER_ROWS
                    r0 = a * PEER_NKEYS + i2
                    xa = at_scr[c % 2, r0:r0 + PEER_ROWS, ls]
                    act = xa + xa * lax.erf(xa)
                    w = jnp.zeros((PEER_ROWS, PEER_LANES), F32)
                    for h in range(PEER_HEADS):
                        p = shared["e1"][h] * e2_scr[h, i2:i2 + PEER_ROWS, ls]
                        w = w + jnp.where(p >= shared["cz"][h], p, 0.0)
                    mt_scr[c % 2, r0:r0 + PEER_ROWS, ls] = (w * act).astype(BF)

                blocks += [functools.partial(block, rb) for rb in range(PEER_NKEYS // PEER_ROWS)]
        return blocks

    for nh in range(tn // PEER_ACT_LANES):
        act_piece(0, nh)
    for c in range(n_chunks):
        pieces = []
        if c + 1 < n_chunks:
            pieces += [functools.partial(act_piece, c + 1, nh) for nh in range(tn // PEER_ACT_LANES)]
        if c >= 1:
            pieces += [functools.partial(value_piece, c - 1, mq) for mq in range(D_MODEL // PEER_VROWS)]
        blocks = gate_blocks(c)
        for bi, blk in enumerate(blocks):
            for pi, piece in enumerate(pieces):
                if (pi * len(blocks)) // len(pieces) == bi:
                    piece()
            blk()
    for mq in range(D_MODEL // PEER_VROWS):
        value_piece(n_chunks - 1, mq)

    @pl.when(j == n_esteps - 1)
    def _():
        h2 = h_ref[...] + acc_scr[...].T
        y_ref[...] = _rms(h2, gf_ref[...])


def _peer_final(h2d, g2, wqt, keys, u_bf, vt_bf, gf):
    rows = h2d.shape[0]
    tn = PEER_TOKENS
    n_esteps = PEER_N_EXPERTS // PEER_EXPERTS
    hs = pl.BlockSpec((tn, D_MODEL), lambda i, j: (i, 0))
    return pl.pallas_call(
        functools.partial(_peer_kernel, n_esteps),
        grid=(rows // tn, n_esteps),
        in_specs=[hs, _const((1, D_MODEL)), _const((2 * PEER_HEADS * PEER_NKEYS, D_MODEL)),
                  _const((2, PEER_NKEYS, PEER_NKEYS)),
                  pl.BlockSpec((PEER_EXPERTS, D_MODEL), lambda i, j: (j, 0)),
                  pl.BlockSpec((D_MODEL, PEER_EXPERTS), lambda i, j: (0, j)), _const((1, D_MODEL))],
        out_specs=hs,
        out_shape=jax.ShapeDtypeStruct((rows, D_MODEL), F32),
        scratch_shapes=[pltpu.VMEM((tn, D_MODEL), BF),
                        pltpu.VMEM((2 * PEER_HEADS * PEER_NKEYS, tn), F32),
                        pltpu.VMEM((2 * PEER_HEADS, PEER_NKEYS, tn), F32),
                        pltpu.VMEM((2 * PEER_HEADS, PEER_SV_ROWS, tn), F32),
                        pltpu.VMEM((PEER_HEADS, PEER_NKEYS, tn), F32),
                        pltpu.VMEM((PEER_HEADS, PEER_NKEYS, tn), F32),
                        pltpu.VMEM((PEER_HEADS, SUBLANES, tn), F32),
                        pltpu.VMEM((2, PEER_CHUNK, tn), F32),
                        pltpu.VMEM((2, PEER_CHUNK, tn), BF),
                        pltpu.VMEM((D_MODEL, tn), F32)],
        compiler_params=_params(("parallel", "arbitrary")),
        name="peer_final",
    )(h2d, g2, wqt, keys, u_bf, vt_bf, gf)


def _block_diag(w):
    s, g, r, c = w.shape
    on_diag = (jnp.arange(g)[:, None] == jnp.arange(g)[None, :])[None, :, None, :, None]
    return jnp.where(on_diag, w[:, :, :, None, :], 0.0).reshape(s, g * r, g * c)


def _row(v):
    return v.reshape(1, -1).astype(F32)


def kernel(x_prompt, x_sample, mem_prompt, state_ssm_re, state_ssm_im, state_conv, cache_mem_k, cache_mem_v,
           norm1_g, w_in, b_in, s5_log_dt, s5_a_re, s5_a_im, s5_b_re, s5_b_im, s5_c_re, s5_c_im, s5_d,
           s5_glu_w, s5_glu_b, w_s5_proj, conv_dw_w, conv_dw_b, conv_ln_g, conv_ln_b, w_conv_proj, mem_norm_g,
           w_mem_k, w_mem_v, w_xa_proj, w_out, norm2_g, peer_w_q, peer_sub_keys, peer_u, peer_v, final_norm_g):
    assert norm1_g.shape[0] == 1, "single-layer trunk"
    bp, tp, _ = x_prompt.shape
    bs, ts, _ = x_sample.shape
    n_s = bs * ts
    gps = S5_GROUPS // S5_SUPER

    w1 = wg = w_in.reshape(D_MODEL, -1).astype(BF)
    b1 = bg = _row(b_in)

    def per_super(w, r, c):
        return _block_diag(jnp.swapaxes(w.reshape(S5_SUPER, gps, r, c), 2, 3))

    sw = {
        "bmat": jnp.concatenate([per_super(s5_b_re, S5_STATE, S5_GROUP), per_super(s5_b_im, S5_STATE, S5_GROUP)],
                                axis=2),
        "cmat": jnp.concatenate([per_super(s5_c_re, S5_GROUP, S5_STATE), -per_super(s5_c_im, S5_GROUP, S5_STATE)],
                                axis=1).astype(BF),
        "dskip": _row(s5_d),
        "ldt": _row(jnp.repeat(s5_log_dt.reshape(-1), S5_STATE)),
        "are": _row(s5_a_re),
        "aim": _row(s5_a_im),
        "dww": jnp.repeat(conv_dw_w.reshape(CONV_K, WIDTH).astype(F32), SUBLANES, axis=0),
        "dwb": _row(conv_dw_b),
        "lng": _row(conv_ln_g),
        "lnb": _row(conv_ln_b),
        "gluw": s5_glu_w.reshape(WIDTH, WIDTH).astype(BF),
        "glub": _row(s5_glu_b),
    }
    g1 = _row(norm1_g)
    merge_w = (g1, wg, bg, w_s5_proj.reshape(WIDTH, D_MODEL).astype(BF), w_conv_proj.reshape(WIDTH, D_MODEL).astype(BF),
               w_xa_proj.reshape(WIDTH, D_MODEL).astype(BF), w_out.reshape(D_MODEL, D_MODEL).astype(BF))
    peer_w = (_row(norm2_g), peer_w_q.reshape(D_MODEL, -1).T.astype(BF),
              peer_sub_keys.reshape(2, PEER_NKEYS, PEER_NKEYS).astype(BF),
              peer_u.reshape(PEER_N_EXPERTS, D_MODEL).astype(BF),
              peer_v.reshape(PEER_N_EXPERTS, D_MODEL).T.astype(BF), _row(final_norm_g))

    mk, mv = _memkv(mem_prompt.reshape(bp * N_MEM, D_MODEL), _row(mem_norm_g),
                    w_mem_k.reshape(D_MODEL, WIDTH).astype(BF), w_mem_v.reshape(D_MODEL, WIDTH).astype(BF))
    tt_p = 64
    zeros_st = jnp.zeros((bp, S5_CH), F32)
    q_p, s5_p, cv_p, a_last, re_p, im_p = _seq(x_prompt, bp, tt_p, 512, zeros_st, zeros_st,
                                               jnp.zeros((HIST * bp, WIDTH), F32), g1, w1, b1, sw)
    xa_p = _attn_prompt(q_p, mk.reshape(bp, N_MEM, WIDTH), mv.reshape(bp, N_MEM, WIDTH))
    h_p = _merge(x_prompt.reshape(bp * tp, D_MODEL), s5_p.reshape(bp * tp, WIDTH), cv_p.reshape(bp * tp, WIDTH),
                 xa_p.reshape(bp * tp, WIDTH), *merge_w)
    y_prompt = _peer_final(h_p, *peer_w).reshape(bp, tp, D_MODEL)
    conv_p = jnp.swapaxes(a_last[(tt_p - HIST) * bp:].reshape(HIST, bp, WIDTH), 0, 1)

    xs2 = x_sample.reshape(n_s, D_MODEL)
    st_conv = state_conv.reshape(bs, HIST, WIDTH)
    hist_s = jnp.swapaxes(st_conv, 0, 1).reshape(HIST * bs, WIDTH)
    q_s, s5_s, cv_s, a_s, re_s, im_s = _seq(xs2, bs, ts, LANES, state_ssm_re.reshape(bs, S5_CH),
                                            state_ssm_im.reshape(bs, S5_CH), hist_s, g1, w1, b1, sw)
    xa_s = _attn_sample(q_s.reshape(bs, ts, WIDTH).astype(F32),
                        cache_mem_k.reshape(bs, N_MEM * XA_HEADS, XA_HEAD_DIM),
                        cache_mem_v.reshape(bs, N_MEM * XA_HEADS, XA_HEAD_DIM))
    h_s = _merge(xs2, s5_s, cv_s, xa_s.reshape(n_s, WIDTH).astype(BF), *merge_w)
    y_sample = _peer_final(h_s, *peer_w).reshape(bs, ts, D_MODEL)
    conv_s = jnp.concatenate([st_conv[:, ts:], jnp.swapaxes(a_s.reshape(ts, bs, WIDTH), 0, 1)], axis=1)

    st_shape_p = (1, bp, S5_GROUPS, S5_STATE)
    st_shape_s = (1, bs, S5_GROUPS, S5_STATE)
    return (y_prompt, y_sample, re_p.reshape(st_shape_p), im_p.reshape(st_shape_p), conv_p[None],
            mk.reshape(1, bp, N_MEM, XA_HEADS, XA_HEAD_DIM), mv.reshape(1, bp, N_MEM, XA_HEADS, XA_HEAD_DIM),
            re_s.reshape(st_shape_s), im_s.reshape(st_shape_s), conv_s[None])
```
